```python
import math
import jax
import jax.numpy as jnp
from jax import lax
import numpy as np

D_MODEL = 1024
BATCH = 1
SEQ = 16384
DEPTH = 2
DEC_BATCH = 32
DEC_SEQ = 1
PAST_LEN = 16384
PAGE_SIZE = 128

H_A = 8
HD_A = 64
D_A = H_A * HD_A
BLOCK_A = 256
N_SEL = 3
Q_BLOCK = 128
D_B = 512
G_B = 8
BW_B = D_B // G_B
CONV_W = 4
C_LRU = 8.0
D_C = 512
G_C = 8
CW_C = D_C // G_C
CHUNK = 128
N_BRANCH = 3
IN_SIZES = (D_A, D_A, D_A, D_A, D_B, D_B, D_C, D_C, D_C, N_BRANCH * D_MODEL)
D_IN = sum(IN_SIZES)
EPS = 1e-6

kernel_name = "hybrid_moba_rglru_chunkmlp_step"


def _rmsnorm(x, w):
    xf = x.astype(jnp.float32)
    y = xf * lax.rsqrt(jnp.mean(xf * xf, axis=-1, keepdims=True) + EPS)
    return (y * w.astype(jnp.float32)).astype(x.dtype)


def _alibi_slopes():
    return jnp.exp2(-8.0 * jnp.arange(1, H_A + 1, dtype=jnp.float32) / H_A)


def _project(x, norm_w, w_in):
    B, T, _ = x.shape
    h = _rmsnorm(x, norm_w)
    z = jnp.einsum('btd,de->bte', h, w_in)
    offs = np.cumsum(IN_SIZES)[:-1].tolist()
    q, k, v, ga, xb, gb, u, vc, gc, gm = jnp.split(z, offs, axis=-1)
    q = q.reshape(B, T, H_A, HD_A)
    k = k.reshape(B, T, H_A, HD_A)
    v = v.reshape(B, T, H_A, HD_A)
    return q, k, v, ga, xb, gb, jax.nn.gelu(u), jax.nn.gelu(vc), gc, gm


def _gather_blocks(blocks, sel):
    return jax.vmap(jax.vmap(lambda kb, s: kb[s]))(blocks, sel)


def _moba_attend(qh, q_pos, sel, valid, k_sel, v_sel, k_own, v_own, own_pos, slopes):
    f32 = jnp.float32
    B, H, Q, N = sel.shape
    scale = HD_A ** -0.5
    sel_pos = sel[..., None] * BLOCK_A + jnp.arange(BLOCK_A, dtype=jnp.int32)
    dist_sel = (q_pos[:, None, None] - sel_pos).astype(f32)
    s_sel = jnp.einsum('bhqd,bhqnkd->bhqnk', qh, k_sel, preferred_element_type=f32) * scale - slopes[None, :, None, None, None] * dist_sel
    s_sel = jnp.where(valid[..., None], s_sel, -jnp.inf).reshape(B, H, Q, N * BLOCK_A)
    dist_own = (q_pos[:, None] - own_pos[None, :]).astype(f32)
    s_own = jnp.einsum('bhqd,bhkd->bhqk', qh, k_own, preferred_element_type=f32) * scale - slopes[None, :, None, None] * dist_own
    s_own = jnp.where(own_pos[None, :] <= q_pos[:, None], s_own, -jnp.inf)
    p = jax.nn.softmax(jnp.concatenate([s_sel, s_own], axis=-1), axis=-1)
    p_sel = p[..., :N * BLOCK_A].reshape(B, H, Q, N, BLOCK_A).astype(v_sel.dtype)
    p_own = p[..., N * BLOCK_A:].astype(v_own.dtype)
    return jnp.einsum('bhqnk,bhqnkd->bhqd', p_sel, v_sel) + jnp.einsum('bhqk,bhkd->bhqd', p_own, v_own)


def _moba_prompt(q, k, v, slopes):
    B, S, H, D = q.shape
    nb = -(-S // BLOCK_A)
    pad = nb * BLOCK_A - S
    qh = q.transpose(0, 2, 1, 3)
    kh = jnp.pad(k.transpose(0, 2, 1, 3), ((0, 0), (0, 0), (0, pad), (0, 0)))
    vh = jnp.pad(v.transpose(0, 2, 1, 3), ((0, 0), (0, 0), (0, pad), (0, 0)))
    k_blocks = kh.reshape(B, H, nb, BLOCK_A, D)
    v_blocks = vh.reshape(B, H, nb, BLOCK_A, D)
    k_mean = jnp.mean(k_blocks.astype(jnp.float32), axis=3)
    n_sel = min(N_SEL, nb)
    blk = jnp.arange(nb, dtype=jnp.int32)
    offs = jnp.arange(Q_BLOCK, dtype=jnp.int32)

    def one_block(i):
        q0 = i * Q_BLOCK
        qb = lax.dynamic_slice_in_dim(qh, q0, Q_BLOCK, axis=2)
        q_pos = q0 + offs
        own = q0 // BLOCK_A
        g = jnp.einsum('bhqd,bhnd->bhqn', qb.astype(jnp.float32), k_mean)
        g = jnp.where(blk < own, g, -jnp.inf)
        _, sel = lax.top_k(g, n_sel)
        valid = jnp.broadcast_to(jnp.arange(n_sel) < own, sel.shape)
        k_sel = _gather_blocks(k_blocks, sel)
        v_sel = _gather_blocks(v_blocks, sel)
        k_own = lax.dynamic_index_in_dim(k_blocks, own, axis=2, keepdims=False)
        v_own = lax.dynamic_index_in_dim(v_blocks, own, axis=2, keepdims=False)
        own_pos = own * BLOCK_A + jnp.arange(BLOCK_A, dtype=jnp.int32)
        return _moba_attend(qb, q_pos, sel, valid, k_sel, v_sel, k_own, v_own, own_pos, slopes)

    out = lax.map(one_block, jnp.arange(S // Q_BLOCK, dtype=jnp.int32))
    return out.transpose(1, 0, 3, 2, 4).reshape(B, S, H * D)


def _moba_sample(q, k, v, cache_k_l, cache_v_l, page_table, slopes):
    B, T, H, D = q.shape
    ppb = BLOCK_A // PAGE_SIZE
    n_full = PAST_LEN // BLOCK_A
    own_start = n_full * BLOCK_A
    own_pages = (PAST_LEN - own_start) // PAGE_SIZE
    qh = q.transpose(0, 2, 1, 3)
    q_pos = PAST_LEN + jnp.arange(T, dtype=jnp.int32)
    pt_own = page_table[:, n_full * ppb:n_full * ppb + own_pages]
    k_own = jnp.concatenate([cache_k_l[pt_own].reshape(B, own_pages * PAGE_SIZE, H, D), k], axis=1).transpose(0, 2, 1, 3)
    v_own = jnp.concatenate([cache_v_l[pt_own].reshape(B, own_pages * PAGE_SIZE, H, D), v], axis=1).transpose(0, 2, 1, 3)
    own_pos = own_start + jnp.arange(own_pages * PAGE_SIZE + T, dtype=jnp.int32)
    if n_full > 0:
        pt_full = page_table[:, :n_full * ppb]
        k_mean = jnp.mean(cache_k_l[pt_full].astype(jnp.float32).reshape(B, n_full, BLOCK_A, H, D), axis=2).transpose(0, 2, 1, 3)
        n_sel = min(N_SEL, n_full)
        g = jnp.einsum('bhqd,bhnd->bhqn', qh.astype(jnp.float32), k_mean)
        _, sel = lax.top_k(g, n_sel)
        valid = jnp.ones(sel.shape, dtype=bool)
        logical = sel[..., None] * ppb + jnp.arange(ppb, dtype=jnp.int32)
        phys = page_table[jnp.arange(B)[:, None, None, None, None], logical]
        h_idx = jnp.arange(H)[None, :, None, None, None, None]
        row = jnp.arange(PAGE_SIZE)
        k_sel = cache_k_l[phys[..., None], row, h_idx].reshape(B, H, T, n_sel, BLOCK_A, D)
        v_sel = cache_v_l[phys[..., None], row, h_idx].reshape(B, H, T, n_sel, BLOCK_A, D)
    else:
        sel = jnp.zeros((B, H, T, 0), dtype=jnp.int32)
        valid = jnp.zeros((B, H, T, 0), dtype=bool)
        k_sel = jnp.zeros((B, H, T, 0, BLOCK_A, D), dtype=k.dtype)
        v_sel = jnp.zeros((B, H, T, 0, BLOCK_A, D), dtype=v.dtype)
    out = _moba_attend(qh, q_pos, sel, valid, k_sel, v_sel, k_own, v_own, own_pos, slopes)
    return out.transpose(0, 2, 1, 3).reshape(B, T, H * D)


def _lru_combine(c1, c2):
    a1, b1 = c1
    a2, b2 = c2
    return a1 * a2, a2 * b1 + b2


def _rglru_branch(xb, conv_prev, conv_w, conv_b, wa, ba, wx, bx, lam, h0):
    f32 = jnp.float32
    B, T, _ = xb.shape
    xp = jnp.concatenate([conv_prev.astype(xb.dtype), xb], axis=1)
    xc = conv_b + xp[:, 0:T] * conv_w[0] + xp[:, 1:T + 1] * conv_w[1] + xp[:, 2:T + 2] * conv_w[2] + xp[:, 3:T + 3] * conv_w[3]
    new_conv = xp[:, T:]
    xg = xc.reshape(B, T, G_B, BW_B)
    r = jax.nn.sigmoid((jnp.einsum('btgi,gij->btgj', xg, wa).reshape(B, T, D_B) + ba).astype(f32))
    i = jax.nn.sigmoid((jnp.einsum('btgi,gij->btgj', xg, wx).reshape(B, T, D_B) + bx).astype(f32))
    log_a = -C_LRU * jax.nn.softplus(-lam.astype(f32)) * r
    a = jnp.exp(log_a)
    b = jnp.sqrt(-jnp.expm1(2.0 * log_a)) * (i * xc.astype(f32))
    a_cum, h = lax.associative_scan(_lru_combine, (a, b), axis=1)
    h = h + a_cum * h0.astype(f32)[:, None]
    return h, h[:, -1].astype(h0.dtype), new_conv


def _chunk_mlp(u, vc, ln_w, ln_b, ws, bs):
    f32 = jnp.float32
    B, T, _ = u.shape
    vf = vc.astype(f32)
    mu = jnp.mean(vf, axis=-1, keepdims=True)
    var = jnp.mean(jnp.square(vf - mu), axis=-1, keepdims=True)
    vn = ((vf - mu) * lax.rsqrt(var + EPS) * ln_w + ln_b).astype(vc.dtype)
    L = min(T, CHUNK)
    n = T // L
    w = jnp.tril(ws[:, :L, :L])
    vg = vn.reshape(B, n, L, G_C, CW_C)
    mix = jnp.einsum('gts,bnsgc->bntgc', w, vg) + bs[:, :L].T[None, None, :, :, None]
    return u * mix.reshape(B, T, D_C), vn


def _merge(x, a, b, c, ga, gb, gc, gm, w_a, w_b, w_c, w_o):
    B, T, _ = x.shape
    dt = x.dtype
    a = a.astype(dt) * jax.nn.silu(ga)
    b = b.astype(dt) * jax.nn.silu(gb)
    c = c.astype(dt) * jax.nn.silu(gc)
    g = jax.nn.sigmoid(gm.astype(jnp.float32)).astype(dt).reshape(B, T, N_BRANCH, D_MODEL)
    m = g[:, :, 0] * (a @ w_a) + g[:, :, 1] * (b @ w_b) + g[:, :, 2] * (c @ w_c)
    return x + m @ w_o


def setup_inputs(seed: int = 0) -> dict:
    key = jax.random.key(seed)
    ks = jax.random.split(key, 32)
    f32 = jnp.float32
    n_pages = PAST_LEN // PAGE_SIZE
    n_pool = (DEC_BATCH * n_pages * 5) // 4

    def nrm(k, shape, scale):
        return jax.random.normal(k, shape, f32) * scale

    perm = jax.random.permutation(ks[4], n_pool)[:DEC_BATCH * n_pages]
    u_lam = jax.random.uniform(ks[17], (DEPTH, D_B), f32, minval=0.9, maxval=0.999)
    p_lam = u_lam ** (1.0 / C_LRU)
    return {
        'x_prompt': nrm(ks[0], (BATCH, SEQ, D_MODEL), 1.0),
        'x_sample': nrm(ks[1], (DEC_BATCH, DEC_SEQ, D_MODEL), 1.0),
        'cache_k': nrm(ks[2], (DEPTH, n_pool, PAGE_SIZE, H_A, HD_A), 1.0),
        'cache_v': nrm(ks[3], (DEPTH, n_pool, PAGE_SIZE, H_A, HD_A), 1.0),
        'page_table': perm.reshape(DEC_BATCH, n_pages).astype(jnp.int32),
        'state_lru_h': nrm(ks[5], (DEPTH, DEC_BATCH, D_B), 0.5),
        'state_conv': nrm(ks[6], (DEPTH, DEC_BATCH, CONV_W - 1, D_B), 1.0),
        'norm_w': 1.0 + nrm(ks[7], (DEPTH, D_MODEL), 0.02),
        'w_in': nrm(ks[8], (DEPTH, D_MODEL, D_IN), D_MODEL ** -0.5),
        'w_br_a': nrm(ks[9], (DEPTH, D_A, D_MODEL), D_A ** -0.5),
        'w_br_b': nrm(ks[10], (DEPTH, D_B, D_MODEL), D_B ** -0.5),
        'w_br_c': nrm(ks[11], (DEPTH, D_C, D_MODEL), D_C ** -0.5),
        'w_out': nrm(ks[12], (DEPTH, D_MODEL, D_MODEL), D_MODEL ** -0.5),
        'conv_w': nrm(ks[13], (DEPTH, CONV_W, D_B), CONV_W ** -0.5),
        'conv_b': nrm(ks[14], (DEPTH, D_B), 0.02),
        'lru_wa': nrm(ks[15], (DEPTH, G_B, BW_B, BW_B), BW_B ** -0.5),
        'lru_ba': nrm(ks[16], (DEPTH, D_B), 0.02),
        'lru_wx': nrm(ks[18], (DEPTH, G_B, BW_B, BW_B), BW_B ** -0.5),
        'lru_bx': nrm(ks[19], (DEPTH, D_B), 0.02),
        'lru_lambda': jnp.log(p_lam) - jnp.log1p(-p_lam),
        'gmlp_ln_w': 1.0 + nrm(ks[20], (DEPTH, D_C), 0.02),
        'gmlp_ln_b': nrm(ks[21], (DEPTH, D_C), 0.02),
        'gmlp_ws': nrm(ks[22], (DEPTH, G_C, CHUNK, CHUNK), CHUNK ** -0.5),
        'gmlp_bs': 1.0 + nrm(ks[23], (DEPTH, G_C, CHUNK), 0.02),
        'final_norm_w': 1.0 + nrm(ks[24], (D_MODEL,), 0.02),
    }


def reference(x_prompt, x_sample, cache_k, cache_v, page_table, state_lru_h, state_conv, norm_w, w_in, w_br_a, w_br_b, w_br_c, w_out, conv_w, conv_b, lru_wa, lru_ba, lru_wx, lru_bx, lru_lambda, gmlp_ln_w, gmlp_ln_b, gmlp_ws, gmlp_bs, final_norm_w):
    slopes = _alibi_slopes()
    xp = x_prompt
    xs = x_sample
    kp_l, vp_l, ks_l, vs_l = [], [], [], []
    hp_l, cp_l, hs_l, cs_l, cv_l = [], [], [], [], []
    for l in range(DEPTH):
        q, k, v, ga, xb, gb, u, vc, gc, gm = _project(xp, norm_w[l], w_in[l])
        attn = _moba_prompt(q, k, v, slopes)
        conv0 = jnp.zeros((xp.shape[0], CONV_W - 1, D_B), xp.dtype)
        h00 = jnp.zeros((xp.shape[0], D_B), xp.dtype)
        hb, h_last, conv_new = _rglru_branch(xb, conv0, conv_w[l], conv_b[l], lru_wa[l], lru_ba[l], lru_wx[l], lru_bx[l], lru_lambda[l], h00)
        cm, _ = _chunk_mlp(u, vc, gmlp_ln_w[l], gmlp_ln_b[l], gmlp_ws[l], gmlp_bs[l])
        xp = _merge(xp, attn, hb, cm, ga, gb, gc, gm, w_br_a[l], w_br_b[l], w_br_c[l], w_out[l])
        kp_l.append(k)
        vp_l.append(v)
        hp_l.append(h_last)
        cp_l.append(conv_new)
        q, k, v, ga, xb, gb, u, vc, gc, gm = _project(xs, norm_w[l], w_in[l])
        attn = _moba_sample(q, k, v, cache_k[l], cache_v[l], page_table, slopes)
        hb, h_last, conv_new = _rglru_branch(xb, state_conv[l], conv_w[l], conv_b[l], lru_wa[l], lru_ba[l], lru_wx[l], lru_bx[l], lru_lambda[l], state_lru_h[l])
        cm, vn = _chunk_mlp(u, vc, gmlp_ln_w[l], gmlp_ln_b[l], gmlp_ws[l], gmlp_bs[l])
        xs = _merge(xs, attn, hb, cm, ga, gb, gc, gm, w_br_a[l], w_br_b[l], w_br_c[l], w_out[l])
        ks_l.append(k)
        vs_l.append(v)
        hs_l.append(h_last)
        cs_l.append(conv_new)
        cv_l.append(vn)
    y_prompt = _rmsnorm(xp, final_norm_w)
    y_sample = _rmsnorm(xs, final_norm_w)
    return (y_prompt, y_sample, jnp.stack(kp_l), jnp.stack(vp_l), jnp.stack(ks_l), jnp.stack(vs_l), jnp.stack(hp_l), jnp.stack(cp_l), jnp.stack(hs_l), jnp.stack(cs_l), jnp.stack(cv_l))
```

```python
import functools

import jax
import jax.numpy as jnp
from jax import lax
from jax.experimental import pallas as pl
from jax.experimental.pallas import tpu as pltpu

F32 = jnp.float32
BF16 = jnp.bfloat16
HIGHEST = lax.Precision.HIGHEST

D_MODEL = 1024
SEQ = 16384
DEPTH = 2
DEC_BATCH = 32
PAST_LEN = 16384
PAGE_SIZE = 128
N_HEADS = 8
HEAD_DIM = 64
D_A = N_HEADS * HEAD_DIM
BLOCK_A = 256
N_SEL = 3
D_B = 512
G_B = 8
C_LRU = 8.0
D_C = 512
G_C = 8
CHUNK = 128
D_IN = 4 * D_A + 2 * D_B + 3 * D_C + 3 * D_MODEL
EPS = 1e-6

N_BLOCKS = SEQ // BLOCK_A
N_PAGES = PAST_LEN // PAGE_SIZE
PAGES_PER_BLOCK = BLOCK_A // PAGE_SIZE
N_PAST_BLOCKS = PAST_LEN // BLOCK_A
NEG = -1e30
ROW_TILE = 256
PAGES_PER_STEP = 16
VMEM_LIMIT = 52 * 1024 * 1024

R_GA, R_XB, R_GB, R_U, R_VC, R_GC, R_GM = 0, 512, 1024, 1536, 2048, 2560, 3072
D_REST = D_IN - 3 * D_A


def _gelu(x):
    return 0.5 * x * (1.0 + jnp.tanh(0.7978845608028654 * (x + 0.044715 * (x * x * x))))


def _sigmoid(x):
    return jax.nn.sigmoid(x)


def _silu(x):
    return x * _sigmoid(x)


def _rms_rows(x, w):
    return x * lax.rsqrt(jnp.mean(x * x, axis=-1, keepdims=True) + EPS) * w


def _proj_kernel(x_ref, nw_ref, w_ref, qT_ref, kf_ref, vf_ref, kb_ref, vT_ref, kmean_ref, rest_ref):
    h = _rms_rows(x_ref[...], nw_ref[...]).astype(BF16)

    def proj(a, b):
        return jnp.dot(h, w_ref[:, a:b], preferred_element_type=F32)

    q = proj(0, D_A)
    qT_ref[...] = q.T
    k = proj(D_A, 2 * D_A)
    kf_ref[...] = k
    kb_ref[0] = k.astype(BF16)
    kmean_ref[0] = jnp.mean(k, axis=0, keepdims=True)
    v = proj(2 * D_A, 3 * D_A)
    vf_ref[...] = v
    vT_ref[0] = v.T.astype(BF16)
    o = 3 * D_A
    rest_ref[:, R_GA:R_U] = proj(o + R_GA, o + R_U)
    rest_ref[:, R_U:R_GC] = _gelu(proj(o + R_U, o + R_GC))
    rest_ref[:, R_GC:D_REST] = proj(o + R_GC, o + D_REST)


def _proj_prompt(x, nw, w_bf):
    n = SEQ // ROW_TILE
    const = lambda i: (0, 0)
    return pl.pallas_call(
        _proj_kernel,
        grid=(n,),
        in_specs=[
            pl.BlockSpec((ROW_TILE, D_MODEL), lambda i: (i, 0)),
            pl.BlockSpec((1, D_MODEL), const),
            pl.BlockSpec((D_MODEL, D_IN), const, pipeline_mode=pl.Buffered(1)),
        ],
        out_specs=[
            pl.BlockSpec((D_A, ROW_TILE), lambda i: (0, i)),
            pl.BlockSpec((ROW_TILE, D_A), lambda i: (i, 0)),
            pl.BlockSpec((ROW_TILE, D_A), lambda i: (i, 0)),
            pl.BlockSpec((1, ROW_TILE, D_A), lambda i: (i, 0, 0)),
            pl.BlockSpec((1, D_A, ROW_TILE), lambda i: (i, 0, 0)),
            pl.BlockSpec((1, 1, D_A), lambda i: (i, 0, 0)),
            pl.BlockSpec((ROW_TILE, D_REST), lambda i: (i, 0)),
        ],
        out_shape=[
            jax.ShapeDtypeStruct((D_A, SEQ), F32),
            jax.ShapeDtypeStruct((SEQ, D_A), F32),
            jax.ShapeDtypeStruct((SEQ, D_A), F32),
            jax.ShapeDtypeStruct((n, ROW_TILE, D_A), BF16),
            jax.ShapeDtypeStruct((n, D_A, ROW_TILE), BF16),
            jax.ShapeDtypeStruct((n, 1, D_A), F32),
            jax.ShapeDtypeStruct((SEQ, D_REST), F32),
        ],
        compiler_params=pltpu.CompilerParams(
            dimension_semantics=("arbitrary",), vmem_limit_bytes=VMEM_LIMIT),
        name="proj_prompt",
    )(x, nw, w_bf)


def _attn_kernel(qT_ref, kb_ref, vT_ref, kmean_ref, alibi_ref, o_ref,
                 qpad_ref, selb_ref, m_ref, l_ref, acc_ref):
    own = pl.program_id(0)
    qT = qT_ref[...]
    blk = lax.broadcasted_iota(jnp.int32, (N_BLOCKS, BLOCK_A), 0)
    half = lax.broadcasted_iota(jnp.int32, (2 * HEAD_DIM, BLOCK_A), 0) // HEAD_DIM

    for h in range(N_HEADS):
        hp, par = divmod(h, 2)
        rows = qT[128 * hp:128 * hp + 128, :]
        qpad = jnp.where(half == par, rows, 0.0)
        g = jnp.dot(kmean_ref[:, 128 * hp:128 * hp + 128], qpad,
                    precision=HIGHEST, preferred_element_type=F32)
        g = jnp.where(blk < own, g, -jnp.inf)
        selb = jnp.full((N_BLOCKS, BLOCK_A), NEG, F32)
        for r in range(N_SEL):
            mx = jnp.max(g, axis=0, keepdims=True)
            idx = jnp.min(jnp.where(g == mx, blk, N_BLOCKS), axis=0, keepdims=True)
            hit = blk == idx
            selb = jnp.where(jnp.logical_and(hit, r < own), 0.0, selb)
            g = jnp.where(hit, -jnp.inf, g)
        selb_ref[h] = selb
        qpad_ref[h] = (qpad * (HEAD_DIM ** -0.5)).astype(BF16)

    t_io = lax.broadcasted_iota(jnp.int32, (BLOCK_A, BLOCK_A), 0)
    u_io = lax.broadcasted_iota(jnp.int32, (BLOCK_A, BLOCK_A), 1)

    def scores(kblk, h):
        hp = h // 2
        s = jnp.dot(kblk[:, 128 * hp:128 * hp + 128], qpad_ref[h], preferred_element_type=F32)
        al = alibi_ref[h]
        return s + jnp.concatenate([al, al], axis=1)

    kblk = kb_ref[own]
    for h in range(N_HEADS):
        s = jnp.where(t_io <= u_io, scores(kblk, h), NEG)
        m_new = jnp.max(s, axis=0, keepdims=True)
        p = jnp.exp(s - m_new)
        m_ref[h] = m_new
        l_ref[h] = jnp.sum(p, axis=0, keepdims=True)
        acc_ref[h] = jnp.dot(vT_ref[own, HEAD_DIM * h:HEAD_DIM * (h + 1), :], p.astype(BF16),
                             preferred_element_type=F32)

    def body(j, carry):
        kblk = kb_ref[j]
        off = ((j - own) * BLOCK_A).astype(F32)
        for h in range(N_HEADS):
            slope = 2.0 ** -(h + 1)
            s = scores(kblk, h) + (selb_ref[h, pl.ds(j, 1), :] + slope * off)
            m_old = m_ref[h]
            m_new = jnp.maximum(m_old, jnp.max(s, axis=0, keepdims=True))
            alpha = jnp.exp(m_old - m_new)
            p = jnp.exp(s - m_new)
            m_ref[h] = m_new
            l_ref[h] = alpha * l_ref[h] + jnp.sum(p, axis=0, keepdims=True)
            acc_ref[h] = alpha * acc_ref[h] + jnp.dot(
                vT_ref[j, HEAD_DIM * h:HEAD_DIM * (h + 1), :], p.astype(BF16),
                preferred_element_type=F32)
        return carry

    lax.fori_loop(0, own, body, 0)

    outT = jnp.concatenate([acc_ref[h] / l_ref[h] for h in range(N_HEADS)], axis=0)
    o_ref[...] = outT.T


def _attn_prompt(qT, kb, vT, kmean, alibi):
    const2 = lambda i: (0, 0)
    const3 = lambda i: (0, 0, 0)
    return pl.pallas_call(
        _attn_kernel,
        grid=(N_BLOCKS,),
        in_specs=[
            pl.BlockSpec((D_A, BLOCK_A), lambda i: (0, i)),
            pl.BlockSpec((N_BLOCKS, BLOCK_A, D_A), const3, pipeline_mode=pl.Buffered(1)),
            pl.BlockSpec((N_BLOCKS, D_A, BLOCK_A), const3, pipeline_mode=pl.Buffered(1)),
            pl.BlockSpec((N_BLOCKS, D_A), const2),
            pl.BlockSpec((N_HEADS, BLOCK_A, 128), const3),
        ],
        out_specs=pl.BlockSpec((BLOCK_A, D_A), lambda i: (i, 0)),
        out_shape=jax.ShapeDtypeStruct((SEQ, D_A), F32),
        scratch_shapes=[
            pltpu.VMEM((N_HEADS, 2 * HEAD_DIM, BLOCK_A), BF16),
            pltpu.VMEM((N_HEADS, N_BLOCKS, BLOCK_A), F32),
            pltpu.VMEM((N_HEADS, 1, BLOCK_A), F32),
            pltpu.VMEM((N_HEADS, 1, BLOCK_A), F32),
            pltpu.VMEM((N_HEADS, HEAD_DIM, BLOCK_A), F32),
        ],
        compiler_params=pltpu.CompilerParams(
            dimension_semantics=("arbitrary",), vmem_limit_bytes=VMEM_LIMIT),
        name="attn_prompt",
    )(qT, kb, vT, kmean, alibi)


def _lru_gates(xc, wa, ba, wx, bx, lam):
    xcb = xc.astype(BF16)
    r = _sigmoid(jnp.dot(xcb, wa, preferred_element_type=F32) + ba)
    ig = _sigmoid(jnp.dot(xcb, wx, preferred_element_type=F32) + bx)
    nl = -lam
    softplus = jnp.maximum(nl, 0.0) + jnp.log1p(jnp.exp(-jnp.abs(nl)))
    log_a = (-C_LRU * softplus) * r
    a = jnp.exp(log_a)
    b = jnp.sqrt(-jnp.tanh(log_a) * (a * a + 1.0)) * (ig * xc)
    return a, b


def _layernorm(v, w, b):
    mu = jnp.mean(v, axis=-1, keepdims=True)
    d = v - mu
    var = jnp.mean(d * d, axis=-1, keepdims=True)
    return d * lax.rsqrt(var + EPS) * w + b


def _merge(x, a, b, c, ga, gb, gc, gm, wba, wbb, wbc, wo):
    pa = jnp.dot((a * _silu(ga)).astype(BF16), wba, preferred_element_type=F32)
    pb = jnp.dot((b * _silu(gb)).astype(BF16), wbb, preferred_element_type=F32)
    pc = jnp.dot((c * _silu(gc)).astype(BF16), wbc, preferred_element_type=F32)
    m = (_sigmoid(gm[:, 0:D_MODEL]) * pa + _sigmoid(gm[:, D_MODEL:2 * D_MODEL]) * pb
         + _sigmoid(gm[:, 2 * D_MODEL:3 * D_MODEL]) * pc)
    return x + jnp.dot(m.astype(BF16), wo, preferred_element_type=F32)


def _mix_kernel(final, x_ref, attn_ref, rest_ref, convw_ref, convb_ref, wa_ref, ba_ref, wx_ref,
                bx_ref, lam_ref, lnw_ref, lnb_ref, ws_ref, bsm_ref, wba_ref, wbb_ref, wbc_ref,
                wo_ref, fnw_ref, out_ref, xbtail_ref, htail_ref, ext_ref, hc_ref):
    i = pl.program_id(0)
    T = ROW_TILE

    @pl.when(i == 0)
    def _():
        ext_ref[0:8, :] = jnp.zeros((8, D_B), F32)
        hc_ref[...] = jnp.zeros((1, D_B), F32)

    xb = rest_ref[:, R_XB:R_XB + D_B]
    ext_ref[8:8 + T, :] = xb
    cw = convw_ref[...]
    xc = (convb_ref[...] + ext_ref[5:5 + T, :] * cw[0:1] + ext_ref[6:6 + T, :] * cw[1:2]
          + ext_ref[7:7 + T, :] * cw[2:3] + xb * cw[3:4])
    ext_ref[0:8, :] = xb[T - 8:T, :]
    xbtail_ref[...] = xb[T - 8:T, :]

    a, b = _lru_gates(xc, wa_ref[...], ba_ref[...], wx_ref[...], bx_ref[...], lam_ref[...])
    row = lax.broadcasted_iota(jnp.int32, (T, D_B), 0)
    d = 1
    while d < T:
        a_sh = jnp.where(row >= d, pltpu.roll(a, d, 0), 1.0)
        b_sh = jnp.where(row >= d, pltpu.roll(b, d, 0), 0.0)
        b = a * b_sh + b
        a = a * a_sh
        d *= 2
    hseq = b + a * hc_ref[...]
    hc_ref[...] = hseq[T - 1:T, :]
    htail_ref[...] = hseq[T - 8:T, :]

    vn = _layernorm(rest_ref[:, R_VC:R_VC + D_C], lnw_ref[...], lnb_ref[...]).astype(BF16)
    lane_g = lax.broadcasted_iota(jnp.int32, (CHUNK, D_C), 1) // (D_C // G_C)
    tri = (lax.broadcasted_iota(jnp.int32, (CHUNK, CHUNK), 0)
           >= lax.broadcasted_iota(jnp.int32, (CHUNK, CHUNK), 1))
    mixes = []
    for c in range(T // CHUNK):
        vch = vn[c * CHUNK:(c + 1) * CHUNK, :]
        mix = jnp.zeros((CHUNK, D_C), F32)
        for g in range(G_C):
            wg = jnp.where(tri, ws_ref[g], 0.0).astype(BF16)
            mix = jnp.where(lane_g == g, jnp.dot(wg, vch, preferred_element_type=F32), mix)
        mixes.append(mix + bsm_ref[...])
    cm = rest_ref[:, R_U:R_U + D_C] * jnp.concatenate(mixes, axis=0)

    xn = _merge(x_ref[...], attn_ref[...], hseq, cm,
                rest_ref[:, R_GA:R_GA + D_A], rest_ref[:, R_GB:R_GB + D_B],
                rest_ref[:, R_GC:R_GC + D_C], rest_ref[:, R_GM:R_GM + 3 * D_MODEL],
                wba_ref[...], wbb_ref[...], wbc_ref[...], wo_ref[...])
    out_ref[...] = _rms_rows(xn, fnw_ref[...]) if final else xn


def _mix_prompt(final, x, attn, rest, lw):
    n = SEQ // ROW_TILE
    row = lambda i: (i, 0)
    c2 = lambda i: (0, 0)
    c3 = lambda i: (0, 0, 0)
    vec = lambda d: pl.BlockSpec((1, d), c2)
    return pl.pallas_call(
        functools.partial(_mix_kernel, final),
        grid=(n,),
        in_specs=[
            pl.BlockSpec((ROW_TILE, D_MODEL), row),
            pl.BlockSpec((ROW_TILE, D_A), row),
            pl.BlockSpec((ROW_TILE, D_REST), row),
            pl.BlockSpec((4, D_B), c2), vec(D_B),
            pl.BlockSpec((D_B, D_B), c2), vec(D_B),
            pl.BlockSpec((D_B, D_B), c2), vec(D_B), vec(D_B),
            vec(D_C), vec(D_C),
            pl.BlockSpec((G_C, CHUNK, CHUNK), c3),
            pl.BlockSpec((CHUNK, D_C), c2),
            pl.BlockSpec((D_A, D_MODEL), c2),
            pl.BlockSpec((D_B, D_MODEL), c2),
            pl.BlockSpec((D_C, D_MODEL), c2),
            pl.BlockSpec((D_MODEL, D_MODEL), c2),
            vec(D_MODEL),
        ],
        out_specs=[
            pl.BlockSpec((ROW_TILE, D_MODEL), row),
            pl.BlockSpec((8, D_B), c2),
            pl.BlockSpec((8, D_B), c2),
        ],
        out_shape=[
            jax.ShapeDtypeStruct((SEQ, D_MODEL), F32),
            jax.ShapeDtypeStruct((8, D_B), F32),
            jax.ShapeDtypeStruct((8, D_B), F32),
        ],
        scratch_shapes=[
            pltpu.VMEM((8 + ROW_TILE, D_B), F32),
            pltpu.VMEM((1, D_B), F32),
        ],
        compiler_params=pltpu.CompilerParams(
            dimension_semantics=("arbitrary",), vmem_limit_bytes=VMEM_LIMIT),
        name="mix_prompt",
    )(x, attn, rest, lw["conv_w"], lw["conv_b"], lw["wa"], lw["ba"], lw["wx"], lw["bx"],
      lw["lam"], lw["ln_w"], lw["ln_b"], lw["ws"], lw["bs_mix"], lw["w_a"], lw["w_b"], lw["w_c"],
      lw["w_o"], lw["fnw"])


S_COLS = 1536


def _sproj_kernel(x_ref, nw_ref, w_ref, z_ref):
    h = _rms_rows(x_ref[...], nw_ref[...]).astype(BF16)
    z_ref[...] = jnp.dot(h, w_ref[...], preferred_element_type=F32)


def _proj_sample(x, nw, w_bf):
    return pl.pallas_call(
        _sproj_kernel,
        grid=(D_IN // S_COLS,),
        in_specs=[
            pl.BlockSpec((DEC_BATCH, D_MODEL), lambda j: (0, 0)),
            pl.BlockSpec((1, D_MODEL), lambda j: (0, 0)),
            pl.BlockSpec((D_MODEL, S_COLS), lambda j: (0, j)),
        ],
        out_specs=pl.BlockSpec((DEC_BATCH, S_COLS), lambda j: (0, j)),
        out_shape=jax.ShapeDtypeStruct((DEC_BATCH, D_IN), F32),
        compiler_params=pltpu.CompilerParams(dimension_semantics=("arbitrary",)),
        name="proj_sample",
    )(x, nw, w_bf)


def _skmean_kernel(pt_ref, q_ref, *refs):
    pages = refs[:PAGES_PER_STEP]
    sel_ref = refs[PAGES_PER_STEP]
    km_ref = refs[PAGES_PER_STEP + 1]
    c = pl.program_id(1)
    nb = PAGES_PER_STEP // PAGES_PER_BLOCK
    for i in range(nb):
        s = jnp.sum(pages[2 * i][...], axis=0) + jnp.sum(pages[2 * i + 1][...], axis=0)
        km_ref[c * nb + i] = s * (1.0 / BLOCK_A)

    @pl.when(c == pl.num_programs(1) - 1)
    def _():
        g = jnp.sum(km_ref[...] * q_ref[...], axis=2, keepdims=True)
        blk = lax.broadcasted_iota(jnp.int32, g.shape, 0)
        for r in range(N_SEL):
            mx = jnp.max(g, axis=0, keepdims=True)
            idx = jnp.min(jnp.where(g == mx, blk, N_PAST_BLOCKS), axis=0, keepdims=True)
            sel_ref[0, r] = jnp.broadcast_to(idx[0], (N_HEADS, 128))
            g = jnp.where(blk == idx, -jnp.inf, g)


def _select_sample(layer, pt_flat, q3, cache_k):
    steps = N_PAGES // PAGES_PER_STEP

    def page_spec(i):
        return pl.BlockSpec(
            (None, None, PAGE_SIZE, N_HEADS, HEAD_DIM),
            lambda b, c, pt: (layer, pt[b * N_PAGES + c * PAGES_PER_STEP + i], 0, 0, 0))

    grid_spec = pltpu.PrefetchScalarGridSpec(
        num_scalar_prefetch=1,
        grid=(DEC_BATCH, steps),
        in_specs=[pl.BlockSpec((1, N_HEADS, HEAD_DIM), lambda b, c, pt: (b, 0, 0))]
        + [page_spec(i) for i in range(PAGES_PER_STEP)],
        out_specs=pl.BlockSpec((1, N_SEL, N_HEADS, 128), lambda b, c, pt: (b, 0, 0, 0)),
        scratch_shapes=[pltpu.VMEM((N_PAST_BLOCKS, N_HEADS, HEAD_DIM), F32)],
    )
    return pl.pallas_call(
        _skmean_kernel,
        grid_spec=grid_spec,
        out_shape=jax.ShapeDtypeStruct((DEC_BATCH, N_SEL, N_HEADS, 128), jnp.int32),
        compiler_params=pltpu.CompilerParams(
            dimension_semantics=("arbitrary", "arbitrary"), vmem_limit_bytes=VMEM_LIMIT),
        name="select_sample",
    )(pt_flat, q3, *([cache_k] * PAGES_PER_STEP))


N_SEL_PAGES = N_SEL * PAGES_PER_BLOCK


def _sattn_kernel(pt_ref, sel_ref, q_ref, kn_ref, vn_ref, *refs):
    kp = refs[:N_SEL_PAGES]
    vp = refs[N_SEL_PAGES:2 * N_SEL_PAGES]
    o_ref = refs[2 * N_SEL_PAGES]
    b = pl.program_id(0)
    h = pl.program_id(1)
    q = q_ref[...] * (HEAD_DIM ** -0.5)
    head = lax.broadcasted_iota(jnp.int32, (1, N_HEADS, 1), 1)
    slope = jnp.exp2(-(head + 1).astype(F32))
    t = lax.broadcasted_iota(jnp.int32, (PAGE_SIZE, 1, 1), 0).astype(F32)

    s_list = []
    for r in range(N_SEL):
        blk = sel_ref[(b * N_SEL + r) * N_HEADS + h]
        for hf in range(PAGES_PER_BLOCK):
            s = jnp.sum(kp[r * PAGES_PER_BLOCK + hf][...] * q, axis=2, keepdims=True)
            start = (PAST_LEN - blk * BLOCK_A - hf * PAGE_SIZE).astype(F32)
            s_list.append(s - slope * (start - t))
    s_own = jnp.sum(kn_ref[...] * q, axis=2, keepdims=True)

    m = s_own
    for s in s_list:
        m = jnp.maximum(m, jnp.max(s, axis=0, keepdims=True))
    p_own = jnp.exp(s_own - m)
    l = p_own
    acc = p_own * vn_ref[...]
    for i, s in enumerate(s_list):
        p = jnp.exp(s - m)
        l = l + jnp.sum(p, axis=0, keepdims=True)
        acc = acc + jnp.sum(p * vp[i][...], axis=0, keepdims=True)
    out = jnp.where(head == h, acc / l, 0.0)

    @pl.when(h == 0)
    def _():
        o_ref[...] = out

    @pl.when(h != 0)
    def _():
        o_ref[...] = o_ref[...] + out


def _attn_sample(layer, pt_flat, sel_flat, q3, kn3, vn3, cache_k, cache_v):
    def page_spec(r, hf):
        return pl.BlockSpec(
            (None, None, PAGE_SIZE, N_HEADS, HEAD_DIM),
            lambda b, h, pt, sel: (
                layer,
                pt[b * N_PAGES + sel[(b * N_SEL + r) * N_HEADS + h] * PAGES_PER_BLOCK + hf],
                0, 0, 0))

    row = pl.BlockSpec((1, N_HEADS, HEAD_DIM), lambda b, h, pt, sel: (b, 0, 0))
    pages = [page_spec(r, hf) for r in range(N_SEL) for hf in range(PAGES_PER_BLOCK)]
    grid_spec = pltpu.PrefetchScalarGridSpec(
        num_scalar_prefetch=2,
        grid=(DEC_BATCH, N_HEADS),
        in_specs=[row, row, row] + pages + pages,
        out_specs=row,
    )
    return pl.pallas_call(
        _sattn_kernel,
        grid_spec=grid_spec,
        out_shape=jax.ShapeDtypeStruct((DEC_BATCH, N_HEADS, HEAD_DIM), F32),
        compiler_params=pltpu.CompilerParams(
            dimension_semantics=("arbitrary", "arbitrary"), vmem_limit_bytes=VMEM_LIMIT),
        name="attn_sample",
    )(pt_flat, sel_flat, q3, kn3, vn3, *([cache_k] * N_SEL_PAGES), *([cache_v] * N_SEL_PAGES))


def _smix_kernel(final, x_ref, attn_ref, z_ref, sc0_ref, sc1_ref, sc2_ref, h0_ref, convw_ref,
                 convb_ref, wa_ref, ba_ref, wx_ref, bx_ref, lam_ref, lnw_ref, lnb_ref, ws0_ref,
                 bs0_ref, wba_ref, wbb_ref, wbc_ref, wo_ref, fnw_ref, out_ref, hnew_ref, vn_ref):
    o = 3 * D_A
    xb = z_ref[:, o + R_XB:o + R_XB + D_B]
    cw = convw_ref[...]
    xc = (convb_ref[...] + sc0_ref[...] * cw[0:1] + sc1_ref[...] * cw[1:2]
          + sc2_ref[...] * cw[2:3] + xb * cw[3:4])
    a, b = _lru_gates(xc, wa_ref[...], ba_ref[...], wx_ref[...], bx_ref[...], lam_ref[...])
    hnew = b + a * h0_ref[...]
    hnew_ref[...] = hnew

    vn = _layernorm(_gelu(z_ref[:, o + R_VC:o + R_VC + D_C]), lnw_ref[...], lnb_ref[...])
    vn_ref[...] = vn
    cm = _gelu(z_ref[:, o + R_U:o + R_U + D_C]) * (ws0_ref[...] * vn + bs0_ref[...])

    xn = _merge(x_ref[...], attn_ref[...], hnew, cm,
                z_ref[:, o + R_GA:o + R_GA + D_A], z_ref[:, o + R_GB:o + R_GB + D_B],
                z_ref[:, o + R_GC:o + R_GC + D_C], z_ref[:, o + R_GM:o + R_GM + 3 * D_MODEL],
                wba_ref[...], wbb_ref[...], wbc_ref[...], wo_ref[...])
    out_ref[...] = _rms_rows(xn, fnw_ref[...]) if final else xn


def _mix_sample(final, x, attn, z, sc0, sc1, sc2, h0, lw):
    return pl.pallas_call(
        functools.partial(_smix_kernel, final),
        out_shape=[
            jax.ShapeDtypeStruct((DEC_BATCH, D_MODEL), F32),
            jax.ShapeDtypeStruct((DEC_BATCH, D_B), F32),
            jax.ShapeDtypeStruct((DEC_BATCH, D_C), F32),
        ],
        compiler_params=pltpu.CompilerParams(vmem_limit_bytes=VMEM_LIMIT),
        name="mix_sample",
    )(x, attn, z, sc0, sc1, sc2, h0, lw["conv_w"], lw["conv_b"], lw["wa"], lw["ba"], lw["wx"],
      lw["bx"], lw["lam"], lw["ln_w"], lw["ln_b"], lw["ws0"], lw["bs0"], lw["w_a"], lw["w_b"],
      lw["w_c"], lw["w_o"], lw["fnw"])


def _block_diag(w):
    g, n, _ = w.shape
    eye = jnp.eye(g, dtype=w.dtype)
    return (w[:, :, None, :] * eye[:, None, :, None]).reshape(g * n, g * n)


def kernel(x_prompt, x_sample, cache_k, cache_v, page_table, state_lru_h, state_conv, norm_w, w_in, w_br_a, w_br_b, w_br_c, w_out, conv_w, conv_b, lru_wa, lru_ba, lru_wx, lru_bx, lru_lambda, gmlp_ln_w, gmlp_ln_b, gmlp_ws, gmlp_bs, final_norm_w):
    pt_flat = page_table.reshape(-1)
    slopes = jnp.exp2(-jnp.arange(1, N_HEADS + 1, dtype=F32))
    alibi = jnp.broadcast_to(
        (slopes[:, None] * jnp.arange(BLOCK_A, dtype=F32)[None, :])[:, :, None],
        (N_HEADS, BLOCK_A, 128))
    fnw = final_norm_w.reshape(1, D_MODEL)

    xp = x_prompt.reshape(SEQ, D_MODEL)
    xs = x_sample.reshape(DEC_BATCH, D_MODEL)
    outs = {k: [] for k in ("kp", "vp", "ks", "vs", "hp", "cp", "hs", "cs", "cv")}
    for l in range(DEPTH):
        final = l == DEPTH - 1
        w_bf = w_in[l].astype(BF16)
        nw = norm_w[l].reshape(1, D_MODEL)
        lw = dict(
            conv_w=conv_w[l], conv_b=conv_b[l].reshape(1, D_B),
            wa=_block_diag(lru_wa[l]).astype(BF16), ba=lru_ba[l].reshape(1, D_B),
            wx=_block_diag(lru_wx[l]).astype(BF16), bx=lru_bx[l].reshape(1, D_B),
            lam=lru_lambda[l].reshape(1, D_B),
            ln_w=gmlp_ln_w[l].reshape(1, D_C), ln_b=gmlp_ln_b[l].reshape(1, D_C),
            ws=gmlp_ws[l], bs_mix=jnp.repeat(gmlp_bs[l].T, D_C // G_C, axis=1),
            ws0=jnp.repeat(gmlp_ws[l][:, 0, 0], D_C // G_C).reshape(1, D_C),
            bs0=jnp.repeat(gmlp_bs[l][:, 0], D_C // G_C).reshape(1, D_C),
            w_a=w_br_a[l].astype(BF16), w_b=w_br_b[l].astype(BF16), w_c=w_br_c[l].astype(BF16),
            w_o=w_out[l].astype(BF16), fnw=fnw)

        qT, kf, vf, kb, vT, kmean, rest = _proj_prompt(xp, nw, w_bf)
        attn = _attn_prompt(qT, kb, vT, kmean.reshape(N_BLOCKS, D_A), alibi)
        xp, xbtail, htail = _mix_prompt(final, xp, attn, rest, lw)
        outs["kp"].append(kf.reshape(1, SEQ, N_HEADS, HEAD_DIM))
        outs["vp"].append(vf.reshape(1, SEQ, N_HEADS, HEAD_DIM))
        outs["hp"].append(htail[7:8])
        outs["cp"].append(xbtail[5:8].reshape(1, 3, D_B))

        z = _proj_sample(xs, nw, w_bf)
        hd = (DEC_BATCH, N_HEADS, HEAD_DIM)
        q3 = z[:, 0:D_A].reshape(hd)
        kn = z[:, D_A:2 * D_A]
        vn = z[:, 2 * D_A:3 * D_A]
        sel = _select_sample(l, pt_flat, q3, cache_k)
        sel_flat = sel[:, :, :, 0].reshape(-1)
        attn_s = _attn_sample(l, pt_flat, sel_flat, q3, kn.reshape(hd), vn.reshape(hd),
                              cache_k, cache_v)
        sc = state_conv[l]
        xs, hnew, cvn = _mix_sample(final, xs, attn_s.reshape(DEC_BATCH, D_A), z,
                                    sc[:, 0], sc[:, 1], sc[:, 2], state_lru_h[l], lw)
        xb_s = z[:, 3 * D_A + R_XB:3 * D_A + R_XB + D_B]
        outs["ks"].append(kn.reshape(DEC_BATCH, 1, N_HEADS, HEAD_DIM))
        outs["vs"].append(vn.reshape(DEC_BATCH, 1, N_HEADS, HEAD_DIM))
        outs["hs"].append(hnew)
        outs["cs"].append(jnp.stack([sc[:, 1], sc[:, 2], xb_s], axis=1))
        outs["cv"].append(cvn.reshape(DEC_BATCH, 1, D_C))

    st = lambda k: jnp.stack(outs[k])
    return (xp.reshape(1, SEQ, D_MODEL), xs.reshape(DEC_BATCH, 1, D_MODEL),
            st("kp"), st("vp"), st("ks"), st("vs"), st("hp"), st("cp"), st("hs"), st("cs"), st("cv"))
```

```python
import functools

import jax
import jax.numpy as jnp
from jax import lax
from jax.experimental import pallas as pl
from jax.experimental.pallas import tpu as pltpu

F32 = jnp.float32
BF16 = jnp.bfloat16
HIGHEST = lax.Precision.HIGHEST

D_MODEL = 1024
SEQ = 16384
DEPTH = 2
DEC_BATCH = 32
PAST_LEN = 16384
PAGE_SIZE = 128
N_HEADS = 8
HEAD_DIM = 64
D_A = N_HEADS * HEAD_DIM
BLOCK_A = 256
N_SEL = 3
D_B = 512
G_B = 8
C_LRU = 8.0
D_C = 512
G_C = 8
CHUNK = 128
D_IN = 4 * D_A + 2 * D_B + 3 * D_C + 3 * D_MODEL
EPS = 1e-6

N_BLOCKS = SEQ // BLOCK_A
N_PAGES = PAST_LEN // PAGE_SIZE
PAGES_PER_BLOCK = BLOCK_A // PAGE_SIZE
N_PAST_BLOCKS = PAST_LEN // BLOCK_A
NEG = -1e30
LOG2E = 1.4426950408889634
ROW_TILE = 256
PAGES_PER_STEP = 32
LANES = 128
VMEM_LIMIT = 52 * 1024 * 1024

R_GA, R_XB, R_GB, R_U, R_VC, R_GC, R_GM = 0, 512, 1024, 1536, 2048, 2560, 3072
D_REST = D_IN - 3 * D_A


def _gelu(x):
    return 0.5 * x * (1.0 + jnp.tanh(0.7978845608028654 * (x + 0.044715 * (x * x * x))))


def _sigmoid(x):
    return jax.nn.sigmoid(x)


def _silu(x):
    return x * _sigmoid(x)


def _rms_rows(x, w):
    return x * lax.rsqrt(jnp.mean(x * x, axis=-1, keepdims=True) + EPS) * w


def _proj_kernel(x_ref, nw_ref, w_ref, qT_ref, kT_ref, vT_ref, kb_ref, vTb_ref, kmean_ref, rest_ref):
    h = _rms_rows(x_ref[...], nw_ref[...]).astype(BF16)

    def proj(a, b):
        return jnp.dot(h, w_ref[:, a:b], preferred_element_type=F32)

    qT_ref[...] = proj(0, D_A).T
    k = proj(D_A, 2 * D_A)
    kT_ref[...] = k.T
    kb_ref[0] = k.astype(BF16)
    kmean_ref[0] = jnp.mean(k, axis=0, keepdims=True)
    vT = proj(2 * D_A, 3 * D_A).T
    vT_ref[...] = vT
    vTb_ref[0] = vT.astype(BF16)
    o = 3 * D_A
    rest_ref[:, R_GA:R_U] = proj(o + R_GA, o + R_U)
    rest_ref[:, R_U:R_GC] = _gelu(proj(o + R_U, o + R_GC))
    rest_ref[:, R_GC:D_REST] = proj(o + R_GC, o + D_REST)


def _proj_prompt(x, nw, w_bf):
    n = SEQ // ROW_TILE
    const = lambda i: (0, 0)
    colblk = pl.BlockSpec((D_A, ROW_TILE), lambda i: (0, i))
    return pl.pallas_call(
        _proj_kernel,
        grid=(n,),
        in_specs=[
            pl.BlockSpec((ROW_TILE, D_MODEL), lambda i: (i, 0)),
            pl.BlockSpec((1, D_MODEL), const),
            pl.BlockSpec((D_MODEL, D_IN), const, pipeline_mode=pl.Buffered(1)),
        ],
        out_specs=[
            colblk, colblk, colblk,
            pl.BlockSpec((1, ROW_TILE, D_A), lambda i: (i, 0, 0)),
            pl.BlockSpec((1, D_A, ROW_TILE), lambda i: (i, 0, 0)),
            pl.BlockSpec((1, 1, D_A), lambda i: (i, 0, 0)),
            pl.BlockSpec((ROW_TILE, D_REST), lambda i: (i, 0)),
        ],
        out_shape=[
            jax.ShapeDtypeStruct((D_A, SEQ), F32),
            jax.ShapeDtypeStruct((D_A, SEQ), F32),
            jax.ShapeDtypeStruct((D_A, SEQ), F32),
            jax.ShapeDtypeStruct((n, ROW_TILE, D_A), BF16),
            jax.ShapeDtypeStruct((n, D_A, ROW_TILE), BF16),
            jax.ShapeDtypeStruct((n, 1, D_A), F32),
            jax.ShapeDtypeStruct((SEQ, D_REST), F32),
        ],
        compiler_params=pltpu.CompilerParams(
            dimension_semantics=("arbitrary",), vmem_limit_bytes=VMEM_LIMIT),
        name="proj_prompt",
    )(x, nw, w_bf)


def _attn_kernel(qT_ref, kb_ref, vT_ref, kmean_ref, alibi_ref, o_ref,
                 qpad_ref, selb_ref, m_ref, l_ref, acc_ref, s_ref, p_ref):
    own = pl.program_id(0)
    qT = qT_ref[...]
    blk = lax.broadcasted_iota(jnp.int32, (N_BLOCKS, BLOCK_A), 0)
    half = lax.broadcasted_iota(jnp.int32, (2 * HEAD_DIM, BLOCK_A), 0) // HEAD_DIM

    for h in range(N_HEADS):
        hp, par = divmod(h, 2)
        rows = qT[LANES * hp:LANES * (hp + 1), :]
        qpad = jnp.where(half == par, rows, 0.0)
        g = jnp.dot(kmean_ref[:, LANES * hp:LANES * (hp + 1)], qpad,
                    precision=HIGHEST, preferred_element_type=F32)
        g = jnp.where(blk < own, g, -jnp.inf)
        selb = jnp.full((N_BLOCKS, BLOCK_A), NEG, F32)
        for r in range(N_SEL):
            mx = jnp.max(g, axis=0, keepdims=True)
            idx = jnp.min(jnp.where(g == mx, blk, N_BLOCKS), axis=0, keepdims=True)
            hit = blk == idx
            selb = jnp.where(jnp.logical_and(hit, r < own), 0.0, selb)
            g = jnp.where(hit, -jnp.inf, g)
        selb_ref[h] = selb
        qpad_ref[h] = (qpad * (HEAD_DIM ** -0.5 * LOG2E)).astype(BF16)

    t_io = lax.broadcasted_iota(jnp.int32, (BLOCK_A, BLOCK_A), 0)
    u_io = lax.broadcasted_iota(jnp.int32, (BLOCK_A, BLOCK_A), 1)

    def block(j, is_own):
        kblk = kb_ref[j]
        off = ((j - own) * BLOCK_A).astype(F32)
        blk_max = []
        for h in range(N_HEADS):
            hp = h // 2
            al = alibi_ref[h]
            s = jnp.dot(kblk[:, LANES * hp:LANES * (hp + 1)], qpad_ref[h],
                        preferred_element_type=F32) + jnp.concatenate([al, al], axis=1)
            if is_own:
                s = jnp.where(t_io <= u_io, s, NEG)
            s_ref[h] = s
            blk_max.append(jnp.max(s, axis=0, keepdims=True))
        alphas = []
        for h in range(N_HEADS):
            if is_own:
                m_new = blk_max[h]
                m_eff = m_new
            else:
                rb = (selb_ref[h, pl.ds(j, 1), :] + (2.0 ** -(h + 1)) * off) * LOG2E
                m_old = m_ref[h]
                m_new = jnp.maximum(m_old, blk_max[h] + rb)
                alphas.append(jnp.exp2(m_old - m_new))
                m_eff = m_new - rb
            p = jnp.exp2(s_ref[h] - m_eff)
            psum = jnp.sum(p, axis=0, keepdims=True)
            l_ref[h] = psum if is_own else alphas[h] * l_ref[h] + psum
            m_ref[h] = m_new
            p_ref[h] = p.astype(BF16)
        for h in range(N_HEADS):
            pv = jnp.dot(vT_ref[j, HEAD_DIM * h:HEAD_DIM * (h + 1), :], p_ref[h],
                         preferred_element_type=F32)
            acc_ref[h] = pv if is_own else alphas[h] * acc_ref[h] + pv

    block(own, True)

    def body(j, carry):
        block(j, False)
        return carry

    lax.fori_loop(0, own, body, 0)

    outT = jnp.concatenate([acc_ref[h] / l_ref[h] for h in range(N_HEADS)], axis=0)
    o_ref[...] = outT.T


def _attn_prompt(qT, kb, vT, kmean, alibi):
    const2 = lambda i: (0, 0)
    const3 = lambda i: (0, 0, 0)
    return pl.pallas_call(
        _attn_kernel,
        grid=(N_BLOCKS,),
        in_specs=[
            pl.BlockSpec((D_A, BLOCK_A), lambda i: (0, i)),
            pl.BlockSpec((N_BLOCKS, BLOCK_A, D_A), const3, pipeline_mode=pl.Buffered(1)),
            pl.BlockSpec((N_BLOCKS, D_A, BLOCK_A), const3, pipeline_mode=pl.Buffered(1)),
            pl.BlockSpec((N_BLOCKS, D_A), const2),
            pl.BlockSpec((N_HEADS, BLOCK_A, LANES), const3),
        ],
        out_specs=pl.BlockSpec((BLOCK_A, D_A), lambda i: (i, 0)),
        out_shape=jax.ShapeDtypeStruct((SEQ, D_A), F32),
        scratch_shapes=[
            pltpu.VMEM((N_HEADS, 2 * HEAD_DIM, BLOCK_A), BF16),
            pltpu.VMEM((N_HEADS, N_BLOCKS, BLOCK_A), F32),
            pltpu.VMEM((N_HEADS, 1, BLOCK_A), F32),
            pltpu.VMEM((N_HEADS, 1, BLOCK_A), F32),
            pltpu.VMEM((N_HEADS, HEAD_DIM, BLOCK_A), F32),
            pltpu.VMEM((N_HEADS, BLOCK_A, BLOCK_A), F32),
            pltpu.VMEM((N_HEADS, BLOCK_A, BLOCK_A), BF16),
        ],
        compiler_params=pltpu.CompilerParams(
            dimension_semantics=("arbitrary",), vmem_limit_bytes=VMEM_LIMIT),
        name="attn_prompt",
    )(qT, kb, vT, kmean, alibi)


def _lru_gates(xc, wa, ba, wx, bx, lam):
    xcb = xc.astype(BF16)
    r = _sigmoid(jnp.dot(xcb, wa, preferred_element_type=F32) + ba)
    ig = _sigmoid(jnp.dot(xcb, wx, preferred_element_type=F32) + bx)
    nl = -lam
    softplus = jnp.maximum(nl, 0.0) + jnp.log1p(jnp.exp(-jnp.abs(nl)))
    log_a = (-C_LRU * softplus) * r
    a = jnp.exp(log_a)
    b = jnp.sqrt(-jnp.tanh(log_a) * (a * a + 1.0)) * (ig * xc)
    return a, b


def _layernorm(v, w, b):
    mu = jnp.mean(v, axis=-1, keepdims=True)
    d = v - mu
    var = jnp.mean(d * d, axis=-1, keepdims=True)
    return d * lax.rsqrt(var + EPS) * w + b


def _merge(x, a, b, c, ga, gb, gc, gm, wba, wbb, wbc, wo):
    pa = jnp.dot((a * _silu(ga)).astype(BF16), wba, preferred_element_type=F32)
    pb = jnp.dot((b * _silu(gb)).astype(BF16), wbb, preferred_element_type=F32)
    pc = jnp.dot((c * _silu(gc)).astype(BF16), wbc, preferred_element_type=F32)
    m = (_sigmoid(gm[:, 0:D_MODEL]) * pa + _sigmoid(gm[:, D_MODEL:2 * D_MODEL]) * pb
         + _sigmoid(gm[:, 2 * D_MODEL:3 * D_MODEL]) * pc)
    return x + jnp.dot(m.astype(BF16), wo, preferred_element_type=F32)


def _mix_kernel(final, x_ref, attn_ref, rest_ref, convw_ref, convb_ref, wa_ref, ba_ref, wx_ref,
                bx_ref, lam_ref, lnw_ref, lnb_ref, ws_ref, bsm_ref, wba_ref, wbb_ref, wbc_ref,
                wo_ref, fnw_ref, out_ref, xbtail_ref, htail_ref, ext_ref, hc_ref):
    i = pl.program_id(0)
    T = ROW_TILE

    @pl.when(i == 0)
    def _():
        ext_ref[0:8, :] = jnp.zeros((8, D_B), F32)
        hc_ref[...] = jnp.zeros((1, D_B), F32)

    xb = rest_ref[:, R_XB:R_XB + D_B]
    ext_ref[8:8 + T, :] = xb
    cw = convw_ref[...]
    xc = (convb_ref[...] + ext_ref[5:5 + T, :] * cw[0:1] + ext_ref[6:6 + T, :] * cw[1:2]
          + ext_ref[7:7 + T, :] * cw[2:3] + xb * cw[3:4])
    ext_ref[0:8, :] = xb[T - 8:T, :]
    xbtail_ref[...] = xb[T - 8:T, :]

    a, b = _lru_gates(xc, wa_ref[...], ba_ref[...], wx_ref[...], bx_ref[...], lam_ref[...])
    row = lax.broadcasted_iota(jnp.int32, (T, D_B), 0)
    d = 1
    while d < T:
        a_sh = jnp.where(row >= d, pltpu.roll(a, d, 0), 1.0)
        b_sh = jnp.where(row >= d, pltpu.roll(b, d, 0), 0.0)
        b = a * b_sh + b
        a = a * a_sh
        d *= 2
    hseq = b + a * hc_ref[...]
    hc_ref[...] = hseq[T - 1:T, :]
    htail_ref[...] = hseq[T - 8:T, :]

    vn = _layernorm(rest_ref[:, R_VC:R_VC + D_C], lnw_ref[...], lnb_ref[...]).astype(BF16)
    lane_g = lax.broadcasted_iota(jnp.int32, (CHUNK, D_C), 1) // (D_C // G_C)
    tri = (lax.broadcasted_iota(jnp.int32, (CHUNK, CHUNK), 0)
           >= lax.broadcasted_iota(jnp.int32, (CHUNK, CHUNK), 1))
    mixes = []
    for c in range(T // CHUNK):
        vch = vn[c * CHUNK:(c + 1) * CHUNK, :]
        mix = jnp.zeros((CHUNK, D_C), F32)
        for g in range(G_C):
            wg = jnp.where(tri, ws_ref[g], 0.0).astype(BF16)
            mix = jnp.where(lane_g == g, jnp.dot(wg, vch, preferred_element_type=F32), mix)
        mixes.append(mix + bsm_ref[...])
    cm = rest_ref[:, R_U:R_U + D_C] * jnp.concatenate(mixes, axis=0)

    xn = _merge(x_ref[...], attn_ref[...], hseq, cm,
                rest_ref[:, R_GA:R_GA + D_A], rest_ref[:, R_GB:R_GB + D_B],
                rest_ref[:, R_GC:R_GC + D_C], rest_ref[:, R_GM:R_GM + 3 * D_MODEL],
                wba_ref[...], wbb_ref[...], wbc_ref[...], wo_ref[...])
    out_ref[...] = _rms_rows(xn, fnw_ref[...]) if final else xn


def _mix_prompt(final, x, attn, rest, lw):
    n = SEQ // ROW_TILE
    row = lambda i: (i, 0)
    c2 = lambda i: (0, 0)
    c3 = lambda i: (0, 0, 0)
    vec = lambda d: pl.BlockSpec((1, d), c2)
    return pl.pallas_call(
        functools.partial(_mix_kernel, final),
        grid=(n,),
        in_specs=[
            pl.BlockSpec((ROW_TILE, D_MODEL), row),
            pl.BlockSpec((ROW_TILE, D_A), row),
            pl.BlockSpec((ROW_TILE, D_REST), row),
            pl.BlockSpec((4, D_B), c2), vec(D_B),
            pl.BlockSpec((D_B, D_B), c2), vec(D_B),
            pl.BlockSpec((D_B, D_B), c2), vec(D_B), vec(D_B),
            vec(D_C), vec(D_C),
            pl.BlockSpec((G_C, CHUNK, CHUNK), c3),
            pl.BlockSpec((CHUNK, D_C), c2),
            pl.BlockSpec((D_A, D_MODEL), c2),
            pl.BlockSpec((D_B, D_MODEL), c2),
            pl.BlockSpec((D_C, D_MODEL), c2),
            pl.BlockSpec((D_MODEL, D_MODEL), c2),
            vec(D_MODEL),
        ],
        out_specs=[
            pl.BlockSpec((ROW_TILE, D_MODEL), row),
            pl.BlockSpec((8, D_B), c2),
            pl.BlockSpec((8, D_B), c2),
        ],
        out_shape=[
            jax.ShapeDtypeStruct((SEQ, D_MODEL), F32),
            jax.ShapeDtypeStruct((8, D_B), F32),
            jax.ShapeDtypeStruct((8, D_B), F32),
        ],
        scratch_shapes=[
            pltpu.VMEM((8 + ROW_TILE, D_B), F32),
            pltpu.VMEM((1, D_B), F32),
        ],
        compiler_params=pltpu.CompilerParams(
            dimension_semantics=("arbitrary",), vmem_limit_bytes=VMEM_LIMIT),
        name="mix_prompt",
    )(x, attn, rest, lw["conv_w"], lw["conv_b"], lw["wa"], lw["ba"], lw["wx"], lw["bx"],
      lw["lam"], lw["ln_w"], lw["ln_b"], lw["ws"], lw["bs_mix"], lw["w_a"], lw["w_b"], lw["w_c"],
      lw["w_o"], lw["fnw"])


S_COLS = 1536


def _sproj_kernel(x_ref, nw_ref, w_ref, z_ref):
    h = _rms_rows(x_ref[...], nw_ref[...]).astype(BF16)
    z_ref[...] = jnp.dot(h, w_ref[...], preferred_element_type=F32)


def _proj_sample(x, nw, w_bf):
    return pl.pallas_call(
        _sproj_kernel,
        grid=(D_IN // S_COLS,),
        in_specs=[
            pl.BlockSpec((DEC_BATCH, D_MODEL), lambda j: (0, 0)),
            pl.BlockSpec((1, D_MODEL), lambda j: (0, 0)),
            pl.BlockSpec((D_MODEL, S_COLS), lambda j: (0, j)),
        ],
        out_specs=pl.BlockSpec((DEC_BATCH, S_COLS), lambda j: (0, j)),
        out_shape=jax.ShapeDtypeStruct((DEC_BATCH, D_IN), F32),
        compiler_params=pltpu.CompilerParams(dimension_semantics=("arbitrary",)),
        name="proj_sample",
    )(x, nw, w_bf)


def _skmean_kernel(pt_ref, q_ref, *refs):
    pages = refs[:PAGES_PER_STEP]
    sel_ref = refs[PAGES_PER_STEP]
    km_ref = refs[PAGES_PER_STEP + 1]
    c = pl.program_id(1)
    nb = PAGES_PER_STEP // PAGES_PER_BLOCK
    lane = lax.broadcasted_iota(jnp.int32, (D_A, LANES), 1)

    @pl.when(c == 0)
    def _():
        km_ref[...] = jnp.zeros((D_A, LANES), F32)

    km = km_ref[...]
    for i in range(nb):
        both = pages[PAGES_PER_BLOCK * i][...] + pages[PAGES_PER_BLOCK * i + 1][...]
        col = jnp.sum(both, axis=1, keepdims=True) * (1.0 / BLOCK_A)
        km = jnp.where(lane == c * nb + i, col, km)
    km_ref[...] = km

    @pl.when(c == pl.num_programs(1) - 1)
    def _():
        prod = km * q_ref[0]
        g = jnp.sum(prod.reshape(N_HEADS, HEAD_DIM, LANES), axis=1)
        blk = lax.broadcasted_iota(jnp.int32, (N_HEADS, LANES), 1)
        g = jnp.where(blk < N_PAST_BLOCKS, g, -jnp.inf)
        for r in range(N_SEL):
            mx = jnp.max(g, axis=1, keepdims=True)
            idx = jnp.min(jnp.where(g == mx, blk, LANES), axis=1, keepdims=True)
            sel_ref[0, r] = jnp.broadcast_to(idx, (N_HEADS, LANES))
            g = jnp.where(blk == idx, -jnp.inf, g)


def _select_sample(layer, pt_flat, q_col, cache_kt):
    steps = N_PAGES // PAGES_PER_STEP

    def page_spec(i):
        return pl.BlockSpec(
            (None, None, D_A, PAGE_SIZE),
            lambda b, c, pt: (layer, pt[b * N_PAGES + c * PAGES_PER_STEP + i], 0, 0))

    grid_spec = pltpu.PrefetchScalarGridSpec(
        num_scalar_prefetch=1,
        grid=(DEC_BATCH, steps),
        in_specs=[pl.BlockSpec((1, D_A, 1), lambda b, c, pt: (b, 0, 0))]
        + [page_spec(i) for i in range(PAGES_PER_STEP)],
        out_specs=pl.BlockSpec((1, N_SEL, N_HEADS, LANES), lambda b, c, pt: (b, 0, 0, 0)),
        scratch_shapes=[pltpu.VMEM((D_A, LANES), F32)],
    )
    return pl.pallas_call(
        _skmean_kernel,
        grid_spec=grid_spec,
        out_shape=jax.ShapeDtypeStruct((DEC_BATCH, N_SEL, N_HEADS, LANES), jnp.int32),
        compiler_params=pltpu.CompilerParams(
            dimension_semantics=("arbitrary", "arbitrary"), vmem_limit_bytes=VMEM_LIMIT),
        name="select_sample",
    )(pt_flat, q_col, *([cache_kt] * PAGES_PER_STEP))


N_SEL_PAGES = N_SEL * PAGES_PER_BLOCK


def _sattn_kernel(pt_ref, sel_ref, q_ref, kn_ref, vn_ref, *refs):
    kp = refs[:N_SEL_PAGES]
    vp = refs[N_SEL_PAGES:2 * N_SEL_PAGES]
    o_ref = refs[2 * N_SEL_PAGES]
    b = pl.program_id(0)
    h = pl.program_id(1)
    q = q_ref[0, 0] * (HEAD_DIM ** -0.5)
    slope = jnp.exp2(-(jnp.full((1, 1), h, jnp.int32) + 1).astype(F32))
    t = lax.broadcasted_iota(jnp.int32, (1, PAGE_SIZE), 1).astype(F32)

    s_list = []
    for r in range(N_SEL):
        blk = sel_ref[(b * N_SEL + r) * N_HEADS + h]
        for hf in range(PAGES_PER_BLOCK):
            s = jnp.sum(kp[r * PAGES_PER_BLOCK + hf][...] * q, axis=0, keepdims=True)
            start = (PAST_LEN - blk * BLOCK_A - hf * PAGE_SIZE).astype(F32)
            s_list.append(s - slope * (start - t))
    s_own = jnp.sum(kn_ref[0, 0] * q, axis=0, keepdims=True)

    m = s_own
    for s in s_list:
        m = jnp.maximum(m, jnp.max(s, axis=1, keepdims=True))
    p_own = jnp.exp(s_own - m)
    l = p_own
    acc = p_own * vn_ref[0, 0]
    for i, s in enumerate(s_list):
        p = jnp.exp(s - m)
        l = l + jnp.sum(p, axis=1, keepdims=True)
        acc = acc + jnp.sum(vp[i][...] * p, axis=1, keepdims=True)
    o_ref[0, 0] = acc / l


def _attn_sample(layer, pt_flat, sel_flat, q4, kn4, vn4, cache_kt, cache_vt):
    def page_spec(r, hf):
        return pl.BlockSpec(
            (None, None, None, HEAD_DIM, PAGE_SIZE),
            lambda b, h, pt, sel: (
                layer,
                pt[b * N_PAGES + sel[(b * N_SEL + r) * N_HEADS + h] * PAGES_PER_BLOCK + hf],
                h, 0, 0))

    col = pl.BlockSpec((1, 1, HEAD_DIM, 1), lambda b, h, pt, sel: (b, h, 0, 0))
    pages = [page_spec(r, hf) for r in range(N_SEL) for hf in range(PAGES_PER_BLOCK)]
    grid_spec = pltpu.PrefetchScalarGridSpec(
        num_scalar_prefetch=2,
        grid=(DEC_BATCH, N_HEADS),
        in_specs=[col, col, col] + pages + pages,
        out_specs=col,
    )
    return pl.pallas_call(
        _sattn_kernel,
        grid_spec=grid_spec,
        out_shape=jax.ShapeDtypeStruct((DEC_BATCH, N_HEADS, HEAD_DIM, 1), F32),
        compiler_params=pltpu.CompilerParams(dimension_semantics=("arbitrary", "arbitrary")),
        name="attn_sample",
    )(pt_flat, sel_flat, q4, kn4, vn4, *([cache_kt] * N_SEL_PAGES), *([cache_vt] * N_SEL_PAGES))


def _smix_kernel(final, x_ref, attn_ref, z_ref, sc0_ref, sc1_ref, sc2_ref, h0_ref, convw_ref,
                 convb_ref, wa_ref, ba_ref, wx_ref, bx_ref, lam_ref, lnw_ref, lnb_ref, ws0_ref,
                 bs0_ref, wba_ref, wbb_ref, wbc_ref, wo_ref, fnw_ref, out_ref, hnew_ref, vn_ref):
    o = 3 * D_A
    xb = z_ref[:, o + R_XB:o + R_XB + D_B]
    cw = convw_ref[...]
    xc = (convb_ref[...] + sc0_ref[...] * cw[0:1] + sc1_ref[...] * cw[1:2]
          + sc2_ref[...] * cw[2:3] + xb * cw[3:4])
    a, b = _lru_gates(xc, wa_ref[...], ba_ref[...], wx_ref[...], bx_ref[...], lam_ref[...])
    hnew = b + a * h0_ref[...]
    hnew_ref[...] = hnew

    vn = _layernorm(_gelu(z_ref[:, o + R_VC:o + R_VC + D_C]), lnw_ref[...], lnb_ref[...])
    vn_ref[...] = vn
    cm = _gelu(z_ref[:, o + R_U:o + R_U + D_C]) * (ws0_ref[...] * vn + bs0_ref[...])

    xn = _merge(x_ref[...], attn_ref[...], hnew, cm,
                z_ref[:, o + R_GA:o + R_GA + D_A], z_ref[:, o + R_GB:o + R_GB + D_B],
                z_ref[:, o + R_GC:o + R_GC + D_C], z_ref[:, o + R_GM:o + R_GM + 3 * D_MODEL],
                wba_ref[...], wbb_ref[...], wbc_ref[...], wo_ref[...])
    out_ref[...] = _rms_rows(xn, fnw_ref[...]) if final else xn


def _mix_sample(final, x, attn, z, sc0, sc1, sc2, h0, lw):
    return pl.pallas_call(
        functools.partial(_smix_kernel, final),
        out_shape=[
            jax.ShapeDtypeStruct((DEC_BATCH, D_MODEL), F32),
            jax.ShapeDtypeStruct((DEC_BATCH, D_B), F32),
            jax.ShapeDtypeStruct((DEC_BATCH, D_C), F32),
        ],
        compiler_params=pltpu.CompilerParams(vmem_limit_bytes=VMEM_LIMIT),
        name="mix_sample",
    )(x, attn, z, sc0, sc1, sc2, h0, lw["conv_w"], lw["conv_b"], lw["wa"], lw["ba"], lw["wx"],
      lw["bx"], lw["lam"], lw["ln_w"], lw["ln_b"], lw["ws0"], lw["bs0"], lw["w_a"], lw["w_b"],
      lw["w_c"], lw["w_o"], lw["fnw"])


def _block_diag(w):
    g, n, _ = w.shape
    eye = jnp.eye(g, dtype=w.dtype)
    return (w[:, :, None, :] * eye[:, None, :, None]).reshape(g * n, g * n)


def _heads_last(xT):
    return xT.reshape(N_HEADS, HEAD_DIM, -1).transpose(2, 0, 1)[None]


def kernel(x_prompt, x_sample, cache_k, cache_v, page_table, state_lru_h, state_conv, norm_w, w_in, w_br_a, w_br_b, w_br_c, w_out, conv_w, conv_b, lru_wa, lru_ba, lru_wx, lru_bx, lru_lambda, gmlp_ln_w, gmlp_ln_b, gmlp_ws, gmlp_bs, final_norm_w):
    n_pool = cache_k.shape[1]
    cache_kt = cache_k.transpose(0, 1, 3, 4, 2)
    cache_vt = cache_v.transpose(0, 1, 3, 4, 2)
    cache_kt2 = cache_kt.reshape(DEPTH, n_pool, D_A, PAGE_SIZE)
    pt_flat = page_table.reshape(-1)
    slopes = jnp.exp2(-jnp.arange(1, N_HEADS + 1, dtype=F32))
    alibi = jnp.broadcast_to(
        (LOG2E * slopes[:, None] * jnp.arange(BLOCK_A, dtype=F32)[None, :])[:, :, None],
        (N_HEADS, BLOCK_A, LANES))
    fnw = final_norm_w.reshape(1, D_MODEL)

    xp = x_prompt.reshape(SEQ, D_MODEL)
    xs = x_sample.reshape(DEC_BATCH, D_MODEL)
    outs = {k: [] for k in ("kp", "vp", "ks", "vs", "hp", "cp", "hs", "cs", "cv")}
    for l in range(DEPTH):
        final = l == DEPTH - 1
        w_bf = w_in[l].astype(BF16)
        nw = norm_w[l].reshape(1, D_MODEL)
        lw = dict(
            conv_w=conv_w[l], conv_b=conv_b[l].reshape(1, D_B),
            wa=_block_diag(lru_wa[l]).astype(BF16), ba=lru_ba[l].reshape(1, D_B),
            wx=_block_diag(lru_wx[l]).astype(BF16), bx=lru_bx[l].reshape(1, D_B),
            lam=lru_lambda[l].reshape(1, D_B),
            ln_w=gmlp_ln_w[l].reshape(1, D_C), ln_b=gmlp_ln_b[l].reshape(1, D_C),
            ws=gmlp_ws[l], bs_mix=jnp.repeat(gmlp_bs[l].T, D_C // G_C, axis=1),
            ws0=jnp.repeat(gmlp_ws[l][:, 0, 0], D_C // G_C).reshape(1, D_C),
            bs0=jnp.repeat(gmlp_bs[l][:, 0], D_C // G_C).reshape(1, D_C),
            w_a=w_br_a[l].astype(BF16), w_b=w_br_b[l].astype(BF16), w_c=w_br_c[l].astype(BF16),
            w_o=w_out[l].astype(BF16), fnw=fnw)

        qT, kT, vT, kb, vTb, kmean, rest = _proj_prompt(xp, nw, w_bf)
        attn = _attn_prompt(qT, kb, vTb, kmean.reshape(N_BLOCKS, D_A), alibi)
        xp, xbtail, htail = _mix_prompt(final, xp, attn, rest, lw)
        outs["kp"].append(_heads_last(kT))
        outs["vp"].append(_heads_last(vT))
        outs["hp"].append(htail[7:8])
        outs["cp"].append(xbtail[5:8].reshape(1, 3, D_B))

        z = _proj_sample(xs, nw, w_bf)
        col4 = (DEC_BATCH, N_HEADS, HEAD_DIM, 1)
        qs = z[:, 0:D_A]
        kn = z[:, D_A:2 * D_A]
        vn = z[:, 2 * D_A:3 * D_A]
        sel = _select_sample(l, pt_flat, qs.reshape(DEC_BATCH, D_A, 1), cache_kt2)
        sel_flat = sel[:, :, :, 0].reshape(-1)
        attn_s = _attn_sample(l, pt_flat, sel_flat, qs.reshape(col4), kn.reshape(col4),
                              vn.reshape(col4), cache_kt, cache_vt)
        sc = state_conv[l]
        xs, hnew, cvn = _mix_sample(final, xs, attn_s.reshape(DEC_BATCH, D_A), z,
                                    sc[:, 0], sc[:, 1], sc[:, 2], state_lru_h[l], lw)
        xb_s = z[:, 3 * D_A + R_XB:3 * D_A + R_XB + D_B]
        outs["ks"].append(kn.reshape(DEC_BATCH, 1, N_HEADS, HEAD_DIM))
        outs["vs"].append(vn.reshape(DEC_BATCH, 1, N_HEADS, HEAD_DIM))
        outs["hs"].append(hnew)
        outs["cs"].append(jnp.stack([sc[:, 1], sc[:, 2], xb_s], axis=1))
        outs["cv"].append(cvn.reshape(DEC_BATCH, 1, D_C))

    st = lambda k: jnp.stack(outs[k])
    return (xp.reshape(1, SEQ, D_MODEL), xs.reshape(DEC_BATCH, 1, D_MODEL),
            st("kp"), st("vp"), st("ks"), st("vs"), st("hp"), st("cp"), st("hs"), st("cs"), st("cv"))
```

```python
import functools

import jax
import jax.numpy as jnp
import numpy as np
from jax import lax
from jax.experimental import pallas as pl
from jax.experimental.pallas import tpu as pltpu

F32 = jnp.float32
BF16 = jnp.bfloat16
HIGHEST = lax.Precision.HIGHEST

D_MODEL = 1024
SEQ = 16384
DEPTH = 2
DEC_BATCH = 32
PAST_LEN = 16384
PAGE_SIZE = 128
N_HEADS = 8
HEAD_DIM = 64
D_A = N_HEADS * HEAD_DIM
BLOCK_A = 256
N_SEL = 3
D_B = 512
G_B = 8
C_LRU = 8.0
D_C = 512
G_C = 8
CHUNK = 128
D_IN = 4 * D_A + 2 * D_B + 3 * D_C + 3 * D_MODEL
EPS = 1e-6

N_BLOCKS = SEQ // BLOCK_A
N_PAGES = PAST_LEN // PAGE_SIZE
PAGES_PER_BLOCK = BLOCK_A // PAGE_SIZE
N_PAST_BLOCKS = PAST_LEN // BLOCK_A
NEG = -1e30
LOG2E = 1.4426950408889634
ROW_TILE = 256
PAGES_PER_STEP = 32
LANES = 128
SUM_ROWS = 16
VMEM_LIMIT = 52 * 1024 * 1024

R_GA, R_XB, R_GB, R_U, R_VC, R_GC, R_GM = 0, 512, 1024, 1536, 2048, 2560, 3072
D_REST = D_IN - 3 * D_A


def _gelu(x):
    return 0.5 * x * (1.0 + jnp.tanh(0.7978845608028654 * (x + 0.044715 * (x * x * x))))


def _sigmoid(x):
    return jax.nn.sigmoid(x)


def _silu(x):
    return x * _sigmoid(x)


def _rms_rows(x, w):
    return x * lax.rsqrt(jnp.mean(x * x, axis=-1, keepdims=True) + EPS) * w


def _proj_kernel(x_ref, nw_ref, w_ref, qT_ref, kT_ref, vT_ref, kb_ref, vTb_ref, kmean_ref, rest_ref):
    h = _rms_rows(x_ref[...], nw_ref[...]).astype(BF16)

    def proj(a, b):
        return jnp.dot(h, w_ref[:, a:b], preferred_element_type=F32)

    qT_ref[...] = proj(0, D_A).T
    k = proj(D_A, 2 * D_A)
    kT_ref[...] = k.T
    kb_ref[0] = k.astype(BF16)
    kmean_ref[0] = jnp.mean(k, axis=0, keepdims=True)
    vT = proj(2 * D_A, 3 * D_A).T
    vT_ref[...] = vT
    vTb_ref[0] = vT.astype(BF16)
    o = 3 * D_A
    rest_ref[:, R_GA:R_U] = proj(o + R_GA, o + R_U)
    rest_ref[:, R_U:R_GC] = _gelu(proj(o + R_U, o + R_GC))
    rest_ref[:, R_GC:D_REST] = proj(o + R_GC, o + D_REST)


def _proj_prompt(x, nw, w_bf):
    n = SEQ // ROW_TILE
    const = lambda i: (0, 0)
    colblk = pl.BlockSpec((D_A, ROW_TILE), lambda i: (0, i))
    return pl.pallas_call(
        _proj_kernel,
        grid=(n,),
        in_specs=[
            pl.BlockSpec((ROW_TILE, D_MODEL), lambda i: (i, 0)),
            pl.BlockSpec((1, D_MODEL), const),
            pl.BlockSpec((D_MODEL, D_IN), const, pipeline_mode=pl.Buffered(1)),
        ],
        out_specs=[
            colblk, colblk, colblk,
            pl.BlockSpec((1, ROW_TILE, D_A), lambda i: (i, 0, 0)),
            pl.BlockSpec((1, D_A, ROW_TILE), lambda i: (i, 0, 0)),
            pl.BlockSpec((1, 1, D_A), lambda i: (i, 0, 0)),
            pl.BlockSpec((ROW_TILE, D_REST), lambda i: (i, 0)),
        ],
        out_shape=[
            jax.ShapeDtypeStruct((D_A, SEQ), F32),
            jax.ShapeDtypeStruct((D_A, SEQ), F32),
            jax.ShapeDtypeStruct((D_A, SEQ), F32),
            jax.ShapeDtypeStruct((n, ROW_TILE, D_A), BF16),
            jax.ShapeDtypeStruct((n, D_A, ROW_TILE), BF16),
            jax.ShapeDtypeStruct((n, 1, D_A), F32),
            jax.ShapeDtypeStruct((SEQ, D_REST), F32),
        ],
        compiler_params=pltpu.CompilerParams(
            dimension_semantics=("arbitrary",), vmem_limit_bytes=VMEM_LIMIT),
        name="proj_prompt",
    )(x, nw, w_bf)


def _bf16_terms(c, n=3):
    terms, rest = [], float(c)
    for _ in range(n):
        t = float(np.asarray(rest, dtype=np.float32).astype(BF16).astype(np.float32))
        terms.append(t)
        rest -= t
    return terms


def _attn_kernel(qT_ref, kb_ref, vT_ref, kmean_ref, tpos_ref, o_ref,
                 qpad_ref, selb_ref, m_ref, acc_ref, sa_ref, sb_ref, p_ref):
    s_refs = (sa_ref, sb_ref)
    own = pl.program_id(0)
    qT = qT_ref[...]
    blk = lax.broadcasted_iota(jnp.int32, (N_BLOCKS, BLOCK_A), 0)
    rid = lax.broadcasted_iota(jnp.int32, (2 * HEAD_DIM, BLOCK_A), 0)
    half = rid // HEAD_DIM

    for h in range(N_HEADS):
        hp, par = divmod(h, 2)
        rows = qT[LANES * hp:LANES * (hp + 1), :]
        qpad = jnp.where(half == par, rows, 0.0)
        g = jnp.dot(kmean_ref[:, LANES * hp:LANES * (hp + 1)], qpad,
                    precision=HIGHEST, preferred_element_type=F32)
        g = jnp.where(blk < own, g, -jnp.inf)
        selb = jnp.full((N_BLOCKS, BLOCK_A), NEG, F32)
        for r in range(N_SEL):
            mx = jnp.max(g, axis=0, keepdims=True)
            idx = jnp.min(jnp.where(g == mx, blk, N_BLOCKS), axis=0, keepdims=True)
            hit = blk == idx
            selb = jnp.where(jnp.logical_and(hit, r < own), 0.0, selb)
            g = jnp.where(hit, -jnp.inf, g)
        selb_ref[h] = selb
        qpad_ref[h, 0:LANES, :] = (qpad * (HEAD_DIM ** -0.5 * LOG2E)).astype(BF16)
        c0, c1, c2 = _bf16_terms(2.0 ** -(h + 1) * LOG2E)
        qpad_ref[h, LANES:2 * LANES, :] = jnp.where(
            rid == 0, c0, jnp.where(rid == 1, c1, jnp.where(rid == 2, c2, 0.0))).astype(BF16)

    t_io = lax.broadcasted_iota(jnp.int32, (BLOCK_A, BLOCK_A), 0)
    u_io = lax.broadcasted_iota(jnp.int32, (BLOCK_A, BLOCK_A), 1)

    def scores(j, slot, is_own):
        kblk = kb_ref[j]
        for h in range(N_HEADS):
            hp = h // 2
            k_aug = jnp.concatenate([kblk[:, LANES * hp:LANES * (hp + 1)], tpos_ref[...]], axis=1)
            s = jnp.dot(k_aug, qpad_ref[h], preferred_element_type=F32)
            if is_own:
                s = jnp.where(t_io <= u_io, s, NEG)
            s_refs[slot][h] = s

    def accumulate(j, slot, is_own):
        off = ((j - own) * BLOCK_A).astype(F32)
        alphas = []
        for h in range(N_HEADS):
            blk_max = jnp.max(s_refs[slot][h], axis=0, keepdims=True)
            if is_own:
                m_new = blk_max
                m_eff = m_new
            else:
                rb = (selb_ref[h, pl.ds(j, 1), :] + (2.0 ** -(h + 1)) * off) * LOG2E
                m_old = m_ref[h]
                m_new = jnp.maximum(m_old, blk_max + rb)
                alphas.append(jnp.exp2(m_old - m_new))
                m_eff = m_new - rb
            m_ref[h] = m_new
            p_ref[h] = jnp.exp2(s_refs[slot][h] - m_eff).astype(BF16)
        ones = jnp.ones((SUM_ROWS, BLOCK_A), BF16)
        for h in range(N_HEADS):
            v_aug = jnp.concatenate([vT_ref[j, HEAD_DIM * h:HEAD_DIM * (h + 1), :], ones], axis=0)
            pv = jnp.dot(v_aug, p_ref[h], preferred_element_type=F32)
            acc_ref[h] = pv if is_own else alphas[h] * acc_ref[h] + pv

    scores(own, 0, True)
    scores(0, 1, False)
    accumulate(own, 0, True)

    def pair(i, carry):
        j = 2 * i
        scores(j + 1, 0, False)
        accumulate(j, 1, False)

        @pl.when(j + 1 < own)
        def _():
            scores(jnp.minimum(j + 2, N_BLOCKS - 1), 1, False)
            accumulate(j + 1, 0, False)

        return carry

    lax.fori_loop(0, (own + 1) // 2, pair, 0)

    outT = jnp.concatenate(
        [acc_ref[h, 0:HEAD_DIM, :] / acc_ref[h, HEAD_DIM:HEAD_DIM + 1, :] for h in range(N_HEADS)],
        axis=0)
    o_ref[...] = outT.T


def _attn_prompt(qT, kb, vT, kmean, tpos):
    const2 = lambda i: (0, 0)
    const3 = lambda i: (0, 0, 0)
    return pl.pallas_call(
        _attn_kernel,
        grid=(N_BLOCKS,),
        in_specs=[
            pl.BlockSpec((D_A, BLOCK_A), lambda i: (0, i)),
            pl.BlockSpec((N_BLOCKS, BLOCK_A, D_A), const3, pipeline_mode=pl.Buffered(1)),
            pl.BlockSpec((N_BLOCKS, D_A, BLOCK_A), const3, pipeline_mode=pl.Buffered(1)),
            pl.BlockSpec((N_BLOCKS, D_A), const2),
            pl.BlockSpec((BLOCK_A, LANES), const2),
        ],
        out_specs=pl.BlockSpec((BLOCK_A, D_A), lambda i: (i, 0)),
        out_shape=jax.ShapeDtypeStruct((SEQ, D_A), F32),
        scratch_shapes=[
            pltpu.VMEM((N_HEADS, 2 * LANES, BLOCK_A), BF16),
            pltpu.VMEM((N_HEADS, N_BLOCKS, BLOCK_A), F32),
            pltpu.VMEM((N_HEADS, 1, BLOCK_A), F32),
            pltpu.VMEM((N_HEADS, HEAD_DIM + SUM_ROWS, BLOCK_A), F32),
            pltpu.VMEM((N_HEADS, BLOCK_A, BLOCK_A), F32),
            pltpu.VMEM((N_HEADS, BLOCK_A, BLOCK_A), F32),
            pltpu.VMEM((N_HEADS, BLOCK_A, BLOCK_A), BF16),
        ],
        compiler_params=pltpu.CompilerParams(
            dimension_semantics=("arbitrary",), vmem_limit_bytes=VMEM_LIMIT),
        name="attn_prompt",
    )(qT, kb, vT, kmean, tpos)


def _lru_gates(xc, wa, ba, wx, bx, lam):
    xcb = xc.astype(BF16)
    r = _sigmoid(jnp.dot(xcb, wa, preferred_element_type=F32) + ba)
    ig = _sigmoid(jnp.dot(xcb, wx, preferred_element_type=F32) + bx)
    nl = -lam
    softplus = jnp.maximum(nl, 0.0) + jnp.log1p(jnp.exp(-jnp.abs(nl)))
    log_a = (-C_LRU * softplus) * r
    a = jnp.exp(log_a)
    b = jnp.sqrt(-jnp.tanh(log_a) * (a * a + 1.0)) * (ig * xc)
    return a, b


def _layernorm(v, w, b):
    mu = jnp.mean(v, axis=-1, keepdims=True)
    d = v - mu
    var = jnp.mean(d * d, axis=-1, keepdims=True)
    return d * lax.rsqrt(var + EPS) * w + b


def _merge(x, a, b, c, ga, gb, gc, gm, wba, wbb, wbc, wo):
    pa = jnp.dot((a * _silu(ga)).astype(BF16), wba, preferred_element_type=F32)
    pb = jnp.dot((b * _silu(gb)).astype(BF16), wbb, preferred_element_type=F32)
    pc = jnp.dot((c * _silu(gc)).astype(BF16), wbc, preferred_element_type=F32)
    m = (_sigmoid(gm[:, 0:D_MODEL]) * pa + _sigmoid(gm[:, D_MODEL:2 * D_MODEL]) * pb
         + _sigmoid(gm[:, 2 * D_MODEL:3 * D_MODEL]) * pc)
    return x + jnp.dot(m.astype(BF16), wo, preferred_element_type=F32)


def _mix_kernel(final, x_ref, attn_ref, rest_ref, convw_ref, convb_ref, wa_ref, ba_ref, wx_ref,
                bx_ref, lam_ref, lnw_ref, lnb_ref, ws_ref, bsm_ref, wba_ref, wbb_ref, wbc_ref,
                wo_ref, fnw_ref, out_ref, xbtail_ref, htail_ref, ext_ref, hc_ref):
    i = pl.program_id(0)
    T = ROW_TILE

    @pl.when(i == 0)
    def _():
        ext_ref[0:8, :] = jnp.zeros((8, D_B), F32)
        hc_ref[...] = jnp.zeros((1, D_B), F32)

    xb = rest_ref[:, R_XB:R_XB + D_B]
    ext_ref[8:8 + T, :] = xb
    cw = convw_ref[...]
    xc = (convb_ref[...] + ext_ref[5:5 + T, :] * cw[0:1] + ext_ref[6:6 + T, :] * cw[1:2]
          + ext_ref[7:7 + T, :] * cw[2:3] + xb * cw[3:4])
    ext_ref[0:8, :] = xb[T - 8:T, :]
    xbtail_ref[...] = xb[T - 8:T, :]

    a, b = _lru_gates(xc, wa_ref[...], ba_ref[...], wx_ref[...], bx_ref[...], lam_ref[...])
    row = lax.broadcasted_iota(jnp.int32, (T, D_B), 0)
    d = 1
    while d < T:
        a_sh = jnp.where(row >= d, pltpu.roll(a, d, 0), 1.0)
        b_sh = jnp.where(row >= d, pltpu.roll(b, d, 0), 0.0)
        b = a * b_sh + b
        a = a * a_sh
        d *= 2
    hseq = b + a * hc_ref[...]
    hc_ref[...] = hseq[T - 1:T, :]
    htail_ref[...] = hseq[T - 8:T, :]

    vn = _layernorm(rest_ref[:, R_VC:R_VC + D_C], lnw_ref[...], lnb_ref[...]).astype(BF16)
    lane_g = lax.broadcasted_iota(jnp.int32, (CHUNK, D_C), 1) // (D_C // G_C)
    tri = (lax.broadcasted_iota(jnp.int32, (CHUNK, CHUNK), 0)
           >= lax.broadcasted_iota(jnp.int32, (CHUNK, CHUNK), 1))
    mixes = []
    for c in range(T // CHUNK):
        vch = vn[c * CHUNK:(c + 1) * CHUNK, :]
        mix = jnp.zeros((CHUNK, D_C), F32)
        for g in range(G_C):
            wg = jnp.where(tri, ws_ref[g], 0.0).astype(BF16)
            mix = jnp.where(lane_g == g, jnp.dot(wg, vch, preferred_element_type=F32), mix)
        mixes.append(mix + bsm_ref[...])
    cm = rest_ref[:, R_U:R_U + D_C] * jnp.concatenate(mixes, axis=0)

    xn = _merge(x_ref[...], attn_ref[...], hseq, cm,
                rest_ref[:, R_GA:R_GA + D_A], rest_ref[:, R_GB:R_GB + D_B],
                rest_ref[:, R_GC:R_GC + D_C], rest_ref[:, R_GM:R_GM + 3 * D_MODEL],
                wba_ref[...], wbb_ref[...], wbc_ref[...], wo_ref[...])
    out_ref[...] = _rms_rows(xn, fnw_ref[...]) if final else xn


def _mix_prompt(final, x, attn, rest, lw):
    n = SEQ // ROW_TILE
    row = lambda i: (i, 0)
    c2 = lambda i: (0, 0)
    c3 = lambda i: (0, 0, 0)
    vec = lambda d: pl.BlockSpec((1, d), c2)
    return pl.pallas_call(
        functools.partial(_mix_kernel, final),
        grid=(n,),
        in_specs=[
            pl.BlockSpec((ROW_TILE, D_MODEL), row),
            pl.BlockSpec((ROW_TILE, D_A), row),
            pl.BlockSpec((ROW_TILE, D_REST), row),
            pl.BlockSpec((4, D_B), c2), vec(D_B),
            pl.BlockSpec((D_B, D_B), c2), vec(D_B),
            pl.BlockSpec((D_B, D_B), c2), vec(D_B), vec(D_B),
            vec(D_C), vec(D_C),
            pl.BlockSpec((G_C, CHUNK, CHUNK), c3),
            pl.BlockSpec((CHUNK, D_C), c2),
            pl.BlockSpec((D_A, D_MODEL), c2),
            pl.BlockSpec((D_B, D_MODEL), c2),
            pl.BlockSpec((D_C, D_MODEL), c2),
            pl.BlockSpec((D_MODEL, D_MODEL), c2),
            vec(D_MODEL),
        ],
        out_specs=[
            pl.BlockSpec((ROW_TILE, D_MODEL), row),
            pl.BlockSpec((8, D_B), c2),
            pl.BlockSpec((8, D_B), c2),
        ],
        out_shape=[
            jax.ShapeDtypeStruct((SEQ, D_MODEL), F32),
            jax.ShapeDtypeStruct((8, D_B), F32),
            jax.ShapeDtypeStruct((8, D_B), F32),
        ],
        scratch_shapes=[
            pltpu.VMEM((8 + ROW_TILE, D_B), F32),
            pltpu.VMEM((1, D_B), F32),
        ],
        compiler_params=pltpu.CompilerParams(
            dimension_semantics=("arbitrary",), vmem_limit_bytes=VMEM_LIMIT),
        name="mix_prompt",
    )(x, attn, rest, lw["conv_w"], lw["conv_b"], lw["wa"], lw["ba"], lw["wx"], lw["bx"],
      lw["lam"], lw["ln_w"], lw["ln_b"], lw["ws"], lw["bs_mix"], lw["w_a"], lw["w_b"], lw["w_c"],
      lw["w_o"], lw["fnw"])


S_COLS = 1536


def _sproj_kernel(x_ref, nw_ref, w_ref, z_ref):
    h = _rms_rows(x_ref[...], nw_ref[...]).astype(BF16)
    z_ref[...] = jnp.dot(h, w_ref[...], preferred_element_type=F32)


def _proj_sample(x, nw, w_bf):
    return pl.pallas_call(
        _sproj_kernel,
        grid=(D_IN // S_COLS,),
        in_specs=[
            pl.BlockSpec((DEC_BATCH, D_MODEL), lambda j: (0, 0)),
            pl.BlockSpec((1, D_MODEL), lambda j: (0, 0)),
            pl.BlockSpec((D_MODEL, S_COLS), lambda j: (0, j)),
        ],
        out_specs=pl.BlockSpec((DEC_BATCH, S_COLS), lambda j: (0, j)),
        out_shape=jax.ShapeDtypeStruct((DEC_BATCH, D_IN), F32),
        compiler_params=pltpu.CompilerParams(dimension_semantics=("arbitrary",)),
        name="proj_sample",
    )(x, nw, w_bf)


def _skmean_kernel(pt_ref, q_ref, *refs):
    pages = refs[:PAGES_PER_STEP]
    sel_ref = refs[PAGES_PER_STEP]
    km_ref = refs[PAGES_PER_STEP + 1]
    c = pl.program_id(1)
    nb = PAGES_PER_STEP // PAGES_PER_BLOCK
    lane = lax.broadcasted_iota(jnp.int32, (D_A, LANES), 1)

    @pl.when(c == 0)
    def _():
        km_ref[...] = jnp.zeros((D_A, LANES), F32)

    km = km_ref[...]
    for i in range(nb):
        both = pages[PAGES_PER_BLOCK * i][...] + pages[PAGES_PER_BLOCK * i + 1][...]
        col = jnp.sum(both, axis=1, keepdims=True) * (1.0 / BLOCK_A)
        km = jnp.where(lane == c * nb + i, col, km)
    km_ref[...] = km

    @pl.when(c == pl.num_programs(1) - 1)
    def _():
        prod = km * q_ref[0]
        g = jnp.sum(prod.reshape(N_HEADS, HEAD_DIM, LANES), axis=1)
        blk = lax.broadcasted_iota(jnp.int32, (N_HEADS, LANES), 1)
        g = jnp.where(blk < N_PAST_BLOCKS, g, -jnp.inf)
        for r in range(N_SEL):
            mx = jnp.max(g, axis=1, keepdims=True)
            idx = jnp.min(jnp.where(g == mx, blk, LANES), axis=1, keepdims=True)
            sel_ref[0, r] = jnp.broadcast_to(idx, (N_HEADS, LANES))
            g = jnp.where(blk == idx, -jnp.inf, g)


def _select_sample(layer, pt_flat, q_col, cache_kt):
    steps = N_PAGES // PAGES_PER_STEP

    def page_spec(i):
        return pl.BlockSpec(
            (None, None, D_A, PAGE_SIZE),
            lambda b, c, pt: (layer, pt[b * N_PAGES + c * PAGES_PER_STEP + i], 0, 0))

    grid_spec = pltpu.PrefetchScalarGridSpec(
        num_scalar_prefetch=1,
        grid=(DEC_BATCH, steps),
        in_specs=[pl.BlockSpec((1, D_A, 1), lambda b, c, pt: (b, 0, 0))]
        + [page_spec(i) for i in range(PAGES_PER_STEP)],
        out_specs=pl.BlockSpec((1, N_SEL, N_HEADS, LANES), lambda b, c, pt: (b, 0, 0, 0)),
        scratch_shapes=[pltpu.VMEM((D_A, LANES), F32)],
    )
    return pl.pallas_call(
        _skmean_kernel,
        grid_spec=grid_spec,
        out_shape=jax.ShapeDtypeStruct((DEC_BATCH, N_SEL, N_HEADS, LANES), jnp.int32),
        compiler_params=pltpu.CompilerParams(
            dimension_semantics=("arbitrary", "arbitrary"), vmem_limit_bytes=VMEM_LIMIT),
        name="select_sample",
    )(pt_flat, q_col, *([cache_kt] * PAGES_PER_STEP))


N_SEL_PAGES = N_SEL * PAGES_PER_BLOCK


def _sattn_kernel(pt_ref, sel_ref, q_ref, kn_ref, vn_ref, *refs):
    kp = refs[:N_SEL_PAGES]
    vp = refs[N_SEL_PAGES:2 * N_SEL_PAGES]
    o_ref = refs[2 * N_SEL_PAGES]
    b = pl.program_id(0)
    h = pl.program_id(1)
    q = q_ref[0, 0] * (HEAD_DIM ** -0.5)
    slope = jnp.exp2(-(jnp.full((1, 1), h, jnp.int32) + 1).astype(F32))
    t = lax.broadcasted_iota(jnp.int32, (1, PAGE_SIZE), 1).astype(F32)

    s_list = []
    for r in range(N_SEL):
        blk = sel_ref[(b * N_SEL + r) * N_HEADS + h]
        for hf in range(PAGES_PER_BLOCK):
            s = jnp.sum(kp[r * PAGES_PER_BLOCK + hf][...] * q, axis=0, keepdims=True)
            start = (PAST_LEN - blk * BLOCK_A - hf * PAGE_SIZE).astype(F32)
            s_list.append(s - slope * (start - t))
    s_own = jnp.sum(kn_ref[0, 0] * q, axis=0, keepdims=True)

    m = s_own
    for s in s_list:
        m = jnp.maximum(m, jnp.max(s, axis=1, keepdims=True))
    p_own = jnp.exp(s_own - m)
    l = p_own
    acc = p_own * vn_ref[0, 0]
    for i, s in enumerate(s_list):
        p = jnp.exp(s - m)
        l = l + jnp.sum(p, axis=1, keepdims=True)
        acc = acc + jnp.sum(vp[i][...] * p, axis=1, keepdims=True)
    o_ref[0, 0] = acc / l


def _attn_sample(layer, pt_flat, sel_flat, q4, kn4, vn4, cache_kt, cache_vt):
    def page_spec(r, hf):
        return pl.BlockSpec(
            (None, None, None, HEAD_DIM, PAGE_SIZE),
            lambda b, h, pt, sel: (
                layer,
                pt[b * N_PAGES + sel[(b * N_SEL + r) * N_HEADS + h] * PAGES_PER_BLOCK + hf],
                h, 0, 0))

    col = pl.BlockSpec((1, 1, HEAD_DIM, 1), lambda b, h, pt, sel: (b, h, 0, 0))
    pages = [page_spec(r, hf) for r in range(N_SEL) for hf in range(PAGES_PER_BLOCK)]
    grid_spec = pltpu.PrefetchScalarGridSpec(
        num_scalar_prefetch=2,
        grid=(DEC_BATCH, N_HEADS),
        in_specs=[col, col, col] + pages + pages,
        out_specs=col,
    )
    return pl.pallas_call(
        _sattn_kernel,
        grid_spec=grid_spec,
        out_shape=jax.ShapeDtypeStruct((DEC_BATCH, N_HEADS, HEAD_DIM, 1), F32),
        compiler_params=pltpu.CompilerParams(dimension_semantics=("arbitrary", "arbitrary")),
        name="attn_sample",
    )(pt_flat, sel_flat, q4, kn4, vn4, *([cache_kt] * N_SEL_PAGES), *([cache_vt] * N_SEL_PAGES))


def _smix_kernel(final, x_ref, attn_ref, z_ref, sc0_ref, sc1_ref, sc2_ref, h0_ref, convw_ref,
                 convb_ref, wa_ref, ba_ref, wx_ref, bx_ref, lam_ref, lnw_ref, lnb_ref, ws0_ref,
                 bs0_ref, wba_ref, wbb_ref, wbc_ref, wo_ref, fnw_ref, out_ref, hnew_ref, vn_ref):
    o = 3 * D_A
    xb = z_ref[:, o + R_XB:o + R_XB + D_B]
    cw = convw_ref[...]
    xc = (convb_ref[...] + sc0_ref[...] * cw[0:1] + sc1_ref[...] * cw[1:2]
          + sc2_ref[...] * cw[2:3] + xb * cw[3:4])
    a, b = _lru_gates(xc, wa_ref[...], ba_ref[...], wx_ref[...], bx_ref[...], lam_ref[...])
    hnew = b + a * h0_ref[...]
    hnew_ref[...] = hnew

    vn = _layernorm(_gelu(z_ref[:, o + R_VC:o + R_VC + D_C]), lnw_ref[...], lnb_ref[...])
    vn_ref[...] = vn
    cm = _gelu(z_ref[:, o + R_U:o + R_U + D_C]) * (ws0_ref[...] * vn + bs0_ref[...])

    xn = _merge(x_ref[...], attn_ref[...], hnew, cm,
                z_ref[:, o + R_GA:o + R_GA + D_A], z_ref[:, o + R_GB:o + R_GB + D_B],
                z_ref[:, o + R_GC:o + R_GC + D_C], z_ref[:, o + R_GM:o + R_GM + 3 * D_MODEL],
                wba_ref[...], wbb_ref[...], wbc_ref[...], wo_ref[...])
    out_ref[...] = _rms_rows(xn, fnw_ref[...]) if final else xn


def _mix_sample(final, x, attn, z, sc0, sc1, sc2, h0, lw):
    return pl.pallas_call(
        functools.partial(_smix_kernel, final),
        out_shape=[
            jax.ShapeDtypeStruct((DEC_BATCH, D_MODEL), F32),
            jax.ShapeDtypeStruct((DEC_BATCH, D_B), F32),
            jax.ShapeDtypeStruct((DEC_BATCH, D_C), F32),
        ],
        compiler_params=pltpu.CompilerParams(vmem_limit_bytes=VMEM_LIMIT),
        name="mix_sample",
    )(x, attn, z, sc0, sc1, sc2, h0, lw["conv_w"], lw["conv_b"], lw["wa"], lw["ba"], lw["wx"],
      lw["bx"], lw["lam"], lw["ln_w"], lw["ln_b"], lw["ws0"], lw["bs0"], lw["w_a"], lw["w_b"],
      lw["w_c"], lw["w_o"], lw["fnw"])


def _block_diag(w):
    g, n, _ = w.shape
    eye = jnp.eye(g, dtype=w.dtype)
    return (w[:, :, None, :] * eye[:, None, :, None]).reshape(g * n, g * n)


def _heads_last(xT):
    return xT.reshape(N_HEADS, HEAD_DIM, -1).transpose(2, 0, 1)[None]


def kernel(x_prompt, x_sample, cache_k, cache_v, page_table, state_lru_h, state_conv, norm_w, w_in, w_br_a, w_br_b, w_br_c, w_out, conv_w, conv_b, lru_wa, lru_ba, lru_wx, lru_bx, lru_lambda, gmlp_ln_w, gmlp_ln_b, gmlp_ws, gmlp_bs, final_norm_w):
    n_pool = cache_k.shape[1]
    cache_kt = cache_k.transpose(0, 1, 3, 4, 2)
    cache_vt = cache_v.transpose(0, 1, 3, 4, 2)
    cache_kt2 = cache_kt.reshape(DEPTH, n_pool, D_A, PAGE_SIZE)
    pt_flat = page_table.reshape(-1)
    tpos = jnp.where(jnp.arange(LANES)[None, :] < 3, jnp.arange(BLOCK_A, dtype=F32)[:, None], 0.0).astype(BF16)
    fnw = final_norm_w.reshape(1, D_MODEL)

    xp = x_prompt.reshape(SEQ, D_MODEL)
    xs = x_sample.reshape(DEC_BATCH, D_MODEL)
    outs = {k: [] for k in ("kp", "vp", "ks", "vs", "hp", "cp", "hs", "cs", "cv")}
    for l in range(DEPTH):
        final = l == DEPTH - 1
        w_bf = w_in[l].astype(BF16)
        nw = norm_w[l].reshape(1, D_MODEL)
        lw = dict(
            conv_w=conv_w[l], conv_b=conv_b[l].reshape(1, D_B),
            wa=_block_diag(lru_wa[l]).astype(BF16), ba=lru_ba[l].reshape(1, D_B),
            wx=_block_diag(lru_wx[l]).astype(BF16), bx=lru_bx[l].reshape(1, D_B),
            lam=lru_lambda[l].reshape(1, D_B),
            ln_w=gmlp_ln_w[l].reshape(1, D_C), ln_b=gmlp_ln_b[l].reshape(1, D_C),
            ws=gmlp_ws[l], bs_mix=jnp.repeat(gmlp_bs[l].T, D_C // G_C, axis=1),
            ws0=jnp.repeat(gmlp_ws[l][:, 0, 0], D_C // G_C).reshape(1, D_C),
            bs0=jnp.repeat(gmlp_bs[l][:, 0], D_C // G_C).reshape(1, D_C),
            w_a=w_br_a[l].astype(BF16), w_b=w_br_b[l].astype(BF16), w_c=w_br_c[l].astype(BF16),
            w_o=w_out[l].astype(BF16), fnw=fnw)

        qT, kT, vT, kb, vTb, kmean, rest = _proj_prompt(xp, nw, w_bf)
        attn = _attn_prompt(qT, kb, vTb, kmean.reshape(N_BLOCKS, D_A), tpos)
        xp, xbtail, htail = _mix_prompt(final, xp, attn, rest, lw)
        outs["kp"].append(_heads_last(kT))
        outs["vp"].append(_heads_last(vT))
        outs["hp"].append(htail[7:8])
        outs["cp"].append(xbtail[5:8].reshape(1, 3, D_B))

        z = _proj_sample(xs, nw, w_bf)
        col4 = (DEC_BATCH, N_HEADS, HEAD_DIM, 1)
        qs = z[:, 0:D_A]
        kn = z[:, D_A:2 * D_A]
        vn = z[:, 2 * D_A:3 * D_A]
        sel = _select_sample(l, pt_flat, qs.reshape(DEC_BATCH, D_A, 1), cache_kt2)
        sel_flat = sel[:, :, :, 0].reshape(-1)
        attn_s = _attn_sample(l, pt_flat, sel_flat, qs.reshape(col4), kn.reshape(col4),
                              vn.reshape(col4), cache_kt, cache_vt)
        sc = state_conv[l]
        xs, hnew, cvn = _mix_sample(final, xs, attn_s.reshape(DEC_BATCH, D_A), z,
                                    sc[:, 0], sc[:, 1], sc[:, 2], state_lru_h[l], lw)
        xb_s = z[:, 3 * D_A + R_XB:3 * D_A + R_XB + D_B]
        outs["ks"].append(kn.reshape(DEC_BATCH, 1, N_HEADS, HEAD_DIM))
        outs["vs"].append(vn.reshape(DEC_BATCH, 1, N_HEADS, HEAD_DIM))
        outs["hs"].append(hnew)
        outs["cs"].append(jnp.stack([sc[:, 1], sc[:, 2], xb_s], axis=1))
        outs["cv"].append(cvn.reshape(DEC_BATCH, 1, D_C))

    st = lambda k: jnp.stack(outs[k])
    return (xp.reshape(1, SEQ, D_MODEL), xs.reshape(DEC_BATCH, 1, D_MODEL),
            st("kp"), st("vp"), st("ks"), st("vs"), st("hp"), st("cp"), st("hs"), st("cs"), st("cv"))
```

```python
import functools

import jax
import jax.numpy as jnp
import numpy as np
from jax import lax
from jax.experimental import pallas as pl
from jax.experimental.pallas import tpu as pltpu

F32 = jnp.float32
BF16 = jnp.bfloat16
HIGHEST = lax.Precision.HIGHEST

D_MODEL = 1024
SEQ = 16384
DEPTH = 2
DEC_BATCH = 32
PAST_LEN = 16384
PAGE_SIZE = 128
N_HEADS = 8
HEAD_DIM = 64
D_A = N_HEADS * HEAD_DIM
BLOCK_A = 256
N_SEL = 3
D_B = 512
G_B = 8
C_LRU = 8.0
D_C = 512
G_C = 8
CHUNK = 128
D_IN = 4 * D_A + 2 * D_B + 3 * D_C + 3 * D_MODEL
EPS = 1e-6

N_BLOCKS = SEQ // BLOCK_A
N_PAGES = PAST_LEN // PAGE_SIZE
PAGES_PER_BLOCK = BLOCK_A // PAGE_SIZE
N_PAST_BLOCKS = PAST_LEN // BLOCK_A
NEG = -1e30
LOG2E = 1.4426950408889634
ROW_TILE = 256
PAGES_PER_STEP = 32
LANES = 128
UNROLL = 4
SUM_ROWS = 16
VMEM_LIMIT = 52 * 1024 * 1024

R_GA, R_XB, R_GB, R_U, R_VC, R_GC, R_GM = 0, 512, 1024, 1536, 2048, 2560, 3072
D_REST = D_IN - 3 * D_A


def _gelu(x):
    return 0.5 * x * (1.0 + jnp.tanh(0.7978845608028654 * (x + 0.044715 * (x * x * x))))


def _sigmoid(x):
    return jax.nn.sigmoid(x)


def _silu(x):
    return x * _sigmoid(x)


def _rms_rows(x, w):
    return x * lax.rsqrt(jnp.mean(x * x, axis=-1, keepdims=True) + EPS) * w


def _proj_kernel(n_prev, x_ref, nw_ref, w_ref, *refs):
    qT_ref, kT_ref, vT_ref, kb_ref, vTb_ref, kmean_ref, rest_ref = refs[n_prev:]
    h = _rms_rows(x_ref[...], nw_ref[...]).astype(BF16)

    def proj(a, b):
        return jnp.dot(h, w_ref[:, a:b], preferred_element_type=F32)

    qT_ref[...] = proj(0, D_A).T
    k = proj(D_A, 2 * D_A)
    kT_ref[...] = k.T
    kb_ref[0] = k.astype(BF16)
    kmean_ref[0] = jnp.mean(k, axis=0, keepdims=True)
    vT = proj(2 * D_A, 3 * D_A).T
    vT_ref[...] = vT
    vTb_ref[0] = vT.astype(BF16)
    o = 3 * D_A
    rest_ref[:, R_GA:R_U] = proj(o + R_GA, o + R_U)
    rest_ref[:, R_U:R_GC] = _gelu(proj(o + R_U, o + R_GC))
    rest_ref[:, R_GC:D_REST] = proj(o + R_GC, o + D_REST)


def _proj_prompt(layer, x, nw, w_bf, kv_stacks):
    n = SEQ // ROW_TILE
    const = lambda i: (0, 0)
    colblk = pl.BlockSpec((D_A, ROW_TILE), lambda i: (0, i))
    stackblk = pl.BlockSpec((None, D_A, ROW_TILE), lambda i: (layer, 0, i))
    stack_shape = jax.ShapeDtypeStruct((DEPTH, D_A, SEQ), F32)
    n_prev = len(kv_stacks)
    return pl.pallas_call(
        functools.partial(_proj_kernel, n_prev),
        grid=(n,),
        in_specs=[
            pl.BlockSpec((ROW_TILE, D_MODEL), lambda i: (i, 0)),
            pl.BlockSpec((1, D_MODEL), const),
            pl.BlockSpec((D_MODEL, D_IN), const, pipeline_mode=pl.Buffered(1)),
        ] + [pl.BlockSpec(memory_space=pl.ANY)] * n_prev,
        input_output_aliases={3 + i: 1 + i for i in range(n_prev)},
        out_specs=[
            colblk, stackblk, stackblk,
            pl.BlockSpec((1, ROW_TILE, D_A), lambda i: (i, 0, 0)),
            pl.BlockSpec((1, D_A, ROW_TILE), lambda i: (i, 0, 0)),
            pl.BlockSpec((1, 1, D_A), lambda i: (i, 0, 0)),
            pl.BlockSpec((ROW_TILE, D_REST), lambda i: (i, 0)),
        ],
        out_shape=[
            jax.ShapeDtypeStruct((D_A, SEQ), F32),
            stack_shape,
            stack_shape,
            jax.ShapeDtypeStruct((n, ROW_TILE, D_A), BF16),
            jax.ShapeDtypeStruct((n, D_A, ROW_TILE), BF16),
            jax.ShapeDtypeStruct((n, 1, D_A), F32),
            jax.ShapeDtypeStruct((SEQ, D_REST), F32),
        ],
        compiler_params=pltpu.CompilerParams(
            dimension_semantics=("arbitrary",), vmem_limit_bytes=VMEM_LIMIT),
        name="proj_prompt",
    )(x, nw, w_bf, *kv_stacks)


def _bf16_terms(c, n=3):
    terms, rest = [], float(c)
    for _ in range(n):
        t = float(np.asarray(rest, dtype=np.float32).astype(BF16).astype(np.float32))
        terms.append(t)
        rest -= t
    return terms


def _attn_kernel(qT_ref, kb_ref, vT_ref, kmean_ref, tpos_ref, o_ref,
                 qpad_ref, selb_ref, m_ref, acc_ref, sa_ref, sb_ref, p_ref):
    s_refs = (sa_ref, sb_ref)
    own = pl.program_id(0)
    qT = qT_ref[...]
    blk = lax.broadcasted_iota(jnp.int32, (N_BLOCKS, BLOCK_A), 0)
    rid = lax.broadcasted_iota(jnp.int32, (2 * HEAD_DIM, BLOCK_A), 0)
    half = rid // HEAD_DIM

    for h in range(N_HEADS):
        hp, par = divmod(h, 2)
        rows = qT[LANES * hp:LANES * (hp + 1), :]
        qpad = jnp.where(half == par, rows, 0.0)
        g = jnp.dot(kmean_ref[:, LANES * hp:LANES * (hp + 1)], qpad,
                    precision=HIGHEST, preferred_element_type=F32)
        g = jnp.where(blk < own, g, -jnp.inf)
        selb = jnp.full((N_BLOCKS, BLOCK_A), NEG, F32)
        for r in range(N_SEL):
            mx = jnp.max(g, axis=0, keepdims=True)
            idx = jnp.min(jnp.where(g == mx, blk, N_BLOCKS), axis=0, keepdims=True)
            hit = blk == idx
            selb = jnp.where(jnp.logical_and(hit, r < own), 0.0, selb)
            g = jnp.where(hit, -jnp.inf, g)
        selb_ref[h] = selb
        qpad_ref[h, 0:LANES, :] = (qpad * (HEAD_DIM ** -0.5 * LOG2E)).astype(BF16)
        c0, c1, c2 = _bf16_terms(2.0 ** -(h + 1) * LOG2E)
        qpad_ref[h, LANES:2 * LANES, :] = jnp.where(
            rid == 0, c0, jnp.where(rid == 1, c1, jnp.where(rid == 2, c2, 0.0))).astype(BF16)

    t_io = lax.broadcasted_iota(jnp.int32, (BLOCK_A, BLOCK_A), 0)
    u_io = lax.broadcasted_iota(jnp.int32, (BLOCK_A, BLOCK_A), 1)

    ones = jnp.ones((SUM_ROWS, BLOCK_A), BF16)

    def scores_head(j, slot, h, is_own=False):
        hp = h // 2
        k_aug = jnp.concatenate([kb_ref[j, :, LANES * hp:LANES * (hp + 1)], tpos_ref[...]], axis=1)
        s = jnp.dot(k_aug, qpad_ref[h], preferred_element_type=F32)
        if is_own:
            s = jnp.where(t_io <= u_io, s, NEG)
        s_refs[slot][h] = s

    def softmax_head(j, slot, h, valid=None):
        blk_max = jnp.max(s_refs[slot][h], axis=0, keepdims=True)
        if valid is None:
            alpha, m_new, m_eff = None, blk_max, blk_max
        else:
            off = ((j - own) * BLOCK_A).astype(F32)
            rb = (selb_ref[h, pl.ds(j, 1), :] + (2.0 ** -(h + 1)) * off) * LOG2E
            rb = jnp.where(valid, rb, NEG)
            m_old = m_ref[h]
            m_new = jnp.maximum(m_old, blk_max + rb)
            alpha = jnp.exp2(m_old - m_new)
            m_eff = m_new - rb
        m_ref[h] = m_new
        p_ref[h] = jnp.exp2(s_refs[slot][h] - m_eff).astype(BF16)
        return alpha

    def pv_head(j, h, alpha):
        v_aug = jnp.concatenate([vT_ref[j, HEAD_DIM * h:HEAD_DIM * (h + 1), :], ones], axis=0)
        pv = jnp.dot(v_aug, p_ref[h], preferred_element_type=F32)
        acc_ref[h] = pv if alpha is None else alpha * acc_ref[h] + pv

    def step(j_next, slot_next, j, slot, valid):
        alphas = [None] * N_HEADS
        for h in range(N_HEADS):
            scores_head(j_next, slot_next, h)
            alphas[h] = softmax_head(j, slot, h, valid)
            if h > 0:
                pv_head(j, h - 1, alphas[h - 1])
        pv_head(j, N_HEADS - 1, alphas[N_HEADS - 1])

    for h in range(N_HEADS):
        scores_head(own, 0, h, is_own=True)
    for h in range(N_HEADS):
        scores_head(0, 1, h)
        softmax_head(own, 0, h)
        pv_head(own, h, None)

    def steps(i, carry):
        for u in range(UNROLL):
            j = UNROLL * i + u
            step(jnp.minimum(j + 1, N_BLOCKS - 1), u % 2,
                 jnp.minimum(j, N_BLOCKS - 1), (u + 1) % 2, j < own)
        return carry

    lax.fori_loop(0, (own + UNROLL - 1) // UNROLL, steps, 0)

    outT = jnp.concatenate(
        [acc_ref[h, 0:HEAD_DIM, :] / acc_ref[h, HEAD_DIM:HEAD_DIM + 1, :] for h in range(N_HEADS)],
        axis=0)
    o_ref[...] = outT.T


def _attn_prompt(qT, kb, vT, kmean, tpos):
    const2 = lambda i: (0, 0)
    const3 = lambda i: (0, 0, 0)
    return pl.pallas_call(
        _attn_kernel,
        grid=(N_BLOCKS,),
        in_specs=[
            pl.BlockSpec((D_A, BLOCK_A), lambda i: (0, i)),
            pl.BlockSpec((N_BLOCKS, BLOCK_A, D_A), const3, pipeline_mode=pl.Buffered(1)),
            pl.BlockSpec((N_BLOCKS, D_A, BLOCK_A), const3, pipeline_mode=pl.Buffered(1)),
            pl.BlockSpec((N_BLOCKS, D_A), const2),
            pl.BlockSpec((BLOCK_A, LANES), const2),
        ],
        out_specs=pl.BlockSpec((BLOCK_A, D_A), lambda i: (i, 0)),
        out_shape=jax.ShapeDtypeStruct((SEQ, D_A), F32),
        scratch_shapes=[
            pltpu.VMEM((N_HEADS, 2 * LANES, BLOCK_A), BF16),
            pltpu.VMEM((N_HEADS, N_BLOCKS, BLOCK_A), F32),
            pltpu.VMEM((N_HEADS, 1, BLOCK_A), F32),
            pltpu.VMEM((N_HEADS, HEAD_DIM + SUM_ROWS, BLOCK_A), F32),
            pltpu.VMEM((N_HEADS, BLOCK_A, BLOCK_A), F32),
            pltpu.VMEM((N_HEADS, BLOCK_A, BLOCK_A), F32),
            pltpu.VMEM((N_HEADS, BLOCK_A, BLOCK_A), BF16),
        ],
        compiler_params=pltpu.CompilerParams(
            dimension_semantics=("arbitrary",), vmem_limit_bytes=VMEM_LIMIT),
        name="attn_prompt",
    )(qT, kb, vT, kmean, tpos)


def _lru_gates(xc, wa, ba, wx, bx, lam):
    xcb = xc.astype(BF16)
    r = _sigmoid(jnp.dot(xcb, wa, preferred_element_type=F32) + ba)
    ig = _sigmoid(jnp.dot(xcb, wx, preferred_element_type=F32) + bx)
    nl = -lam
    softplus = jnp.maximum(nl, 0.0) + jnp.log1p(jnp.exp(-jnp.abs(nl)))
    log_a = (-C_LRU * softplus) * r
    a = jnp.exp(log_a)
    b = jnp.sqrt(-jnp.tanh(log_a) * (a * a + 1.0)) * (ig * xc)
    return a, b


def _layernorm(v, w, b):
    mu = jnp.mean(v, axis=-1, keepdims=True)
    d = v - mu
    var = jnp.mean(d * d, axis=-1, keepdims=True)
    return d * lax.rsqrt(var + EPS) * w + b


def _merge(x, a, b, c, ga, gb, gc, gm, wba, wbb, wbc, wo):
    pa = jnp.dot((a * _silu(ga)).astype(BF16), wba, preferred_element_type=F32)
    pb = jnp.dot((b * _silu(gb)).astype(BF16), wbb, preferred_element_type=F32)
    pc = jnp.dot((c * _silu(gc)).astype(BF16), wbc, preferred_element_type=F32)
    m = (_sigmoid(gm[:, 0:D_MODEL]) * pa + _sigmoid(gm[:, D_MODEL:2 * D_MODEL]) * pb
         + _sigmoid(gm[:, 2 * D_MODEL:3 * D_MODEL]) * pc)
    return x + jnp.dot(m.astype(BF16), wo, preferred_element_type=F32)


def _mix_kernel(final, x_ref, attn_ref, rest_ref, convw_ref, convb_ref, wa_ref, ba_ref, wx_ref,
                bx_ref, lam_ref, lnw_ref, lnb_ref, ws_ref, bsm_ref, wba_ref, wbb_ref, wbc_ref,
                wo_ref, fnw_ref, out_ref, xbtail_ref, htail_ref, ext_ref, hc_ref):
    i = pl.program_id(0)
    T = ROW_TILE

    @pl.when(i == 0)
    def _():
        ext_ref[0:8, :] = jnp.zeros((8, D_B), F32)
        hc_ref[...] = jnp.zeros((1, D_B), F32)

    xb = rest_ref[:, R_XB:R_XB + D_B]
    ext_ref[8:8 + T, :] = xb
    cw = convw_ref[...]
    xc = (convb_ref[...] + ext_ref[5:5 + T, :] * cw[0:1] + ext_ref[6:6 + T, :] * cw[1:2]
          + ext_ref[7:7 + T, :] * cw[2:3] + xb * cw[3:4])
    ext_ref[0:8, :] = xb[T - 8:T, :]
    xbtail_ref[...] = xb[T - 8:T, :]

    a, b = _lru_gates(xc, wa_ref[...], ba_ref[...], wx_ref[...], bx_ref[...], lam_ref[...])
    row = lax.broadcasted_iota(jnp.int32, (T, D_B), 0)
    d = 1
    while d < T:
        a_sh = jnp.where(row >= d, pltpu.roll(a, d, 0), 1.0)
        b_sh = jnp.where(row >= d, pltpu.roll(b, d, 0), 0.0)
        b = a * b_sh + b
        a = a * a_sh
        d *= 2
    hseq = b + a * hc_ref[...]
    hc_ref[...] = hseq[T - 1:T, :]
    htail_ref[...] = hseq[T - 8:T, :]

    vn = _layernorm(rest_ref[:, R_VC:R_VC + D_C], lnw_ref[...], lnb_ref[...]).astype(BF16)
    lane_g = lax.broadcasted_iota(jnp.int32, (CHUNK, D_C), 1) // (D_C // G_C)
    tri = (lax.broadcasted_iota(jnp.int32, (CHUNK, CHUNK), 0)
           >= lax.broadcasted_iota(jnp.int32, (CHUNK, CHUNK), 1))
    mixes = []
    for c in range(T // CHUNK):
        vch = vn[c * CHUNK:(c + 1) * CHUNK, :]
        mix = jnp.zeros((CHUNK, D_C), F32)
        for g in range(G_C):
            wg = jnp.where(tri, ws_ref[g], 0.0).astype(BF16)
            mix = jnp.where(lane_g == g, jnp.dot(wg, vch, preferred_element_type=F32), mix)
        mixes.append(mix + bsm_ref[...])
    cm = rest_ref[:, R_U:R_U + D_C] * jnp.concatenate(mixes, axis=0)

    xn = _merge(x_ref[...], attn_ref[...], hseq, cm,
                rest_ref[:, R_GA:R_GA + D_A], rest_ref[:, R_GB:R_GB + D_B],
                rest_ref[:, R_GC:R_GC + D_C], rest_ref[:, R_GM:R_GM + 3 * D_MODEL],
                wba_ref[...], wbb_ref[...], wbc_ref[...], wo_ref[...])
    out_ref[...] = _rms_rows(xn, fnw_ref[...]) if final else xn


def _mix_prompt(final, x, attn, rest, lw):
    n = SEQ // ROW_TILE
    row = lambda i: (i, 0)
    c2 = lambda i: (0, 0)
    c3 = lambda i: (0, 0, 0)
    vec = lambda d: pl.BlockSpec((1, d), c2)
    return pl.pallas_call(
        functools.partial(_mix_kernel, final),
        grid=(n,),
        in_specs=[
            pl.BlockSpec((ROW_TILE, D_MODEL), row),
            pl.BlockSpec((ROW_TILE, D_A), row),
            pl.BlockSpec((ROW_TILE, D_REST), row),
            pl.BlockSpec((4, D_B), c2), vec(D_B),
            pl.BlockSpec((D_B, D_B), c2), vec(D_B),
            pl.BlockSpec((D_B, D_B), c2), vec(D_B), vec(D_B),
            vec(D_C), vec(D_C),
            pl.BlockSpec((G_C, CHUNK, CHUNK), c3),
            pl.BlockSpec((CHUNK, D_C), c2),
            pl.BlockSpec((D_A, D_MODEL), c2),
            pl.BlockSpec((D_B, D_MODEL), c2),
            pl.BlockSpec((D_C, D_MODEL), c2),
            pl.BlockSpec((D_MODEL, D_MODEL), c2),
            vec(D_MODEL),
        ],
        out_specs=[
            pl.BlockSpec((ROW_TILE, D_MODEL), row),
            pl.BlockSpec((8, D_B), c2),
            pl.BlockSpec((8, D_B), c2),
        ],
        out_shape=[
            jax.ShapeDtypeStruct((SEQ, D_MODEL), F32),
            jax.ShapeDtypeStruct((8, D_B), F32),
            jax.ShapeDtypeStruct((8, D_B), F32),
        ],
        scratch_shapes=[
            pltpu.VMEM((8 + ROW_TILE, D_B), F32),
            pltpu.VMEM((1, D_B), F32),
        ],
        compiler_params=pltpu.CompilerParams(
            dimension_semantics=("arbitrary",), vmem_limit_bytes=VMEM_LIMIT),
        name="mix_prompt",
    )(x, attn, rest, lw["conv_w"], lw["conv_b"], lw["wa"], lw["ba"], lw["wx"], lw["bx"],
      lw["lam"], lw["ln_w"], lw["ln_b"], lw["ws"], lw["bs_mix"], lw["w_a"], lw["w_b"], lw["w_c"],
      lw["w_o"], lw["fnw"])


S_COLS = 1536


def _sproj_kernel(x_ref, nw_ref, w_ref, z_ref):
    h = _rms_rows(x_ref[...], nw_ref[...]).astype(BF16)
    z_ref[...] = jnp.dot(h, w_ref[...], preferred_element_type=F32)


def _proj_sample(x, nw, w_bf):
    return pl.pallas_call(
        _sproj_kernel,
        grid=(D_IN // S_COLS,),
        in_specs=[
            pl.BlockSpec((DEC_BATCH, D_MODEL), lambda j: (0, 0)),
            pl.BlockSpec((1, D_MODEL), lambda j: (0, 0)),
            pl.BlockSpec((D_MODEL, S_COLS), lambda j: (0, j)),
        ],
        out_specs=pl.BlockSpec((DEC_BATCH, S_COLS), lambda j: (0, j)),
        out_shape=jax.ShapeDtypeStruct((DEC_BATCH, D_IN), F32),
        compiler_params=pltpu.CompilerParams(dimension_semantics=("arbitrary",)),
        name="proj_sample",
    )(x, nw, w_bf)


def _skmean_kernel(pt_ref, q_ref, *refs):
    pages = refs[:PAGES_PER_STEP]
    sel_ref = refs[PAGES_PER_STEP]
    km_ref = refs[PAGES_PER_STEP + 1]
    c = pl.program_id(1)
    nb = PAGES_PER_STEP // PAGES_PER_BLOCK
    lane = lax.broadcasted_iota(jnp.int32, (D_A, LANES), 1)

    @pl.when(c == 0)
    def _():
        km_ref[...] = jnp.zeros((D_A, LANES), F32)

    km = km_ref[...]
    for i in range(nb):
        both = pages[PAGES_PER_BLOCK * i][...] + pages[PAGES_PER_BLOCK * i + 1][...]
        col = jnp.sum(both, axis=1, keepdims=True) * (1.0 / BLOCK_A)
        km = jnp.where(lane == c * nb + i, col, km)
    km_ref[...] = km

    @pl.when(c == pl.num_programs(1) - 1)
    def _():
        prod = km * q_ref[0]
        g = jnp.sum(prod.reshape(N_HEADS, HEAD_DIM, LANES), axis=1)
        blk = lax.broadcasted_iota(jnp.int32, (N_HEADS, LANES), 1)
        g = jnp.where(blk < N_PAST_BLOCKS, g, -jnp.inf)
        for r in range(N_SEL):
            mx = jnp.max(g, axis=1, keepdims=True)
            idx = jnp.min(jnp.where(g == mx, blk, LANES), axis=1, keepdims=True)
            sel_ref[0, r] = jnp.broadcast_to(idx, (N_HEADS, LANES))
            g = jnp.where(blk == idx, -jnp.inf, g)


def _select_sample(layer, pt_flat, q_col, cache_kt):
    steps = N_PAGES // PAGES_PER_STEP

    def page_spec(i):
        return pl.BlockSpec(
            (None, None, D_A, PAGE_SIZE),
            lambda b, c, pt: (layer, pt[b * N_PAGES + c * PAGES_PER_STEP + i], 0, 0))

    grid_spec = pltpu.PrefetchScalarGridSpec(
        num_scalar_prefetch=1,
        grid=(DEC_BATCH, steps),
        in_specs=[pl.BlockSpec((1, D_A, 1), lambda b, c, pt: (b, 0, 0))]
        + [page_spec(i) for i in range(PAGES_PER_STEP)],
        out_specs=pl.BlockSpec((1, N_SEL, N_HEADS, LANES), lambda b, c, pt: (b, 0, 0, 0)),
        scratch_shapes=[pltpu.VMEM((D_A, LANES), F32)],
    )
    return pl.pallas_call(
        _skmean_kernel,
        grid_spec=grid_spec,
        out_shape=jax.ShapeDtypeStruct((DEC_BATCH, N_SEL, N_HEADS, LANES), jnp.int32),
        compiler_params=pltpu.CompilerParams(
            dimension_semantics=("arbitrary", "arbitrary"), vmem_limit_bytes=VMEM_LIMIT),
        name="select_sample",
    )(pt_flat, q_col, *([cache_kt] * PAGES_PER_STEP))


N_SEL_PAGES = N_SEL * PAGES_PER_BLOCK


def _sattn_kernel(pt_ref, sel_ref, q_ref, kn_ref, vn_ref, *refs):
    kp = refs[:N_SEL_PAGES]
    vp = refs[N_SEL_PAGES:2 * N_SEL_PAGES]
    o_ref = refs[2 * N_SEL_PAGES]
    b = pl.program_id(0)
    h = pl.program_id(1)
    q = q_ref[0] * (HEAD_DIM ** -0.5)
    slope = jnp.exp2(-(jnp.full((1, 1), h, jnp.int32) + 1).astype(F32))
    t = lax.broadcasted_iota(jnp.int32, (1, PAGE_SIZE), 1).astype(F32)

    s_list = []
    for r in range(N_SEL):
        blk = sel_ref[(b * N_SEL + r) * N_HEADS + h]
        for hf in range(PAGES_PER_BLOCK):
            s = jnp.sum(kp[r * PAGES_PER_BLOCK + hf][...] * q, axis=0, keepdims=True)
            start = (PAST_LEN - blk * BLOCK_A - hf * PAGE_SIZE).astype(F32)
            s_list.append(s - slope * (start - t))
    s_own = jnp.sum(kn_ref[0] * q, axis=0, keepdims=True)

    m = s_own
    for s in s_list:
        m = jnp.maximum(m, jnp.max(s, axis=1, keepdims=True))
    p_own = jnp.exp(s_own - m)
    l = p_own
    acc = p_own * vn_ref[0]
    for i, s in enumerate(s_list):
        p = jnp.exp(s - m)
        l = l + jnp.sum(p, axis=1, keepdims=True)
        acc = acc + jnp.sum(vp[i][...] * p, axis=1, keepdims=True)
    o_ref[0, 0] = acc / l


def _attn_sample(layer, pt_flat, sel_flat, qkv_col, cache_kt, cache_vt):
    def page_spec(r, hf):
        return pl.BlockSpec(
            (None, None, None, HEAD_DIM, PAGE_SIZE),
            lambda b, h, pt, sel: (
                layer,
                pt[b * N_PAGES + sel[(b * N_SEL + r) * N_HEADS + h] * PAGES_PER_BLOCK + hf],
                h, 0, 0))

    def col(part):
        return pl.BlockSpec((1, HEAD_DIM, 1), lambda b, h, pt, sel: (b, part * N_HEADS + h, 0))

    pages = [page_spec(r, hf) for r in range(N_SEL) for hf in range(PAGES_PER_BLOCK)]
    grid_spec = pltpu.PrefetchScalarGridSpec(
        num_scalar_prefetch=2,
        grid=(DEC_BATCH, N_HEADS),
        in_specs=[col(0), col(1), col(2)] + pages + pages,
        out_specs=pl.BlockSpec((1, 1, HEAD_DIM, 1), lambda b, h, pt, sel: (b, h, 0, 0)),
    )
    return pl.pallas_call(
        _sattn_kernel,
        grid_spec=grid_spec,
        out_shape=jax.ShapeDtypeStruct((DEC_BATCH, N_HEADS, HEAD_DIM, 1), F32),
        compiler_params=pltpu.CompilerParams(dimension_semantics=("arbitrary", "arbitrary")),
        name="attn_sample",
    )(pt_flat, sel_flat, qkv_col, qkv_col, qkv_col,
      *([cache_kt] * N_SEL_PAGES), *([cache_vt] * N_SEL_PAGES))


def _smix_kernel(final, x_ref, attn_ref, z_ref, sc0_ref, sc1_ref, sc2_ref, h0_ref, convw_ref,
                 convb_ref, wa_ref, ba_ref, wx_ref, bx_ref, lam_ref, lnw_ref, lnb_ref, ws0_ref,
                 bs0_ref, wba_ref, wbb_ref, wbc_ref, wo_ref, fnw_ref, out_ref, hnew_ref, vn_ref):
    o = 3 * D_A
    xb = z_ref[:, o + R_XB:o + R_XB + D_B]
    cw = convw_ref[...]
    xc = (convb_ref[...] + sc0_ref[...] * cw[0:1] + sc1_ref[...] * cw[1:2]
          + sc2_ref[...] * cw[2:3] + xb * cw[3:4])
    a, b = _lru_gates(xc, wa_ref[...], ba_ref[...], wx_ref[...], bx_ref[...], lam_ref[...])
    hnew = b + a * h0_ref[...]
    hnew_ref[...] = hnew

    vn = _layernorm(_gelu(z_ref[:, o + R_VC:o + R_VC + D_C]), lnw_ref[...], lnb_ref[...])
    vn_ref[...] = vn
    cm = _gelu(z_ref[:, o + R_U:o + R_U + D_C]) * (ws0_ref[...] * vn + bs0_ref[...])

    xn = _merge(x_ref[...], attn_ref[...], hnew, cm,
                z_ref[:, o + R_GA:o + R_GA + D_A], z_ref[:, o + R_GB:o + R_GB + D_B],
                z_ref[:, o + R_GC:o + R_GC + D_C], z_ref[:, o + R_GM:o + R_GM + 3 * D_MODEL],
                wba_ref[...], wbb_ref[...], wbc_ref[...], wo_ref[...])
    out_ref[...] = _rms_rows(xn, fnw_ref[...]) if final else xn


def _mix_sample(final, x, attn, z, sc0, sc1, sc2, h0, lw):
    return pl.pallas_call(
        functools.partial(_smix_kernel, final),
        out_shape=[
            jax.ShapeDtypeStruct((DEC_BATCH, D_MODEL), F32),
            jax.ShapeDtypeStruct((DEC_BATCH, D_B), F32),
            jax.ShapeDtypeStruct((DEC_BATCH, D_C), F32),
        ],
        compiler_params=pltpu.CompilerParams(vmem_limit_bytes=VMEM_LIMIT),
        name="mix_sample",
    )(x, attn, z, sc0, sc1, sc2, h0, lw["conv_w"], lw["conv_b"], lw["wa"], lw["ba"], lw["wx"],
      lw["bx"], lw["lam"], lw["ln_w"], lw["ln_b"], lw["ws0"], lw["bs0"], lw["w_a"], lw["w_b"],
      lw["w_c"], lw["w_o"], lw["fnw"])


def _block_diag(w):
    g, n, _ = w.shape
    eye = jnp.eye(g, dtype=w.dtype)
    return (w[:, :, None, :] * eye[:, None, :, None]).reshape(g * n, g * n)


def _heads_last(xT):
    return xT.reshape(DEPTH, N_HEADS, HEAD_DIM, -1).transpose(0, 3, 1, 2)[:, None]


def kernel(x_prompt, x_sample, cache_k, cache_v, page_table, state_lru_h, state_conv, norm_w, w_in, w_br_a, w_br_b, w_br_c, w_out, conv_w, conv_b, lru_wa, lru_ba, lru_wx, lru_bx, lru_lambda, gmlp_ln_w, gmlp_ln_b, gmlp_ws, gmlp_bs, final_norm_w):
    n_pool = cache_k.shape[1]
    cache_kt = cache_k.transpose(0, 1, 3, 4, 2)
    cache_vt = cache_v.transpose(0, 1, 3, 4, 2)
    cache_kt2 = cache_kt.reshape(DEPTH, n_pool, D_A, PAGE_SIZE)
    pt_flat = page_table.reshape(-1)
    tpos = jnp.where(jnp.arange(LANES)[None, :] < 3, jnp.arange(BLOCK_A, dtype=F32)[:, None], 0.0).astype(BF16)
    fnw = final_norm_w.reshape(1, D_MODEL)

    xp = x_prompt.reshape(SEQ, D_MODEL)
    xs = x_sample.reshape(DEC_BATCH, D_MODEL)
    outs = {k: [] for k in ("ks", "vs", "hp", "cp", "hs", "cs", "cv")}
    kv_stacks = ()
    for l in range(DEPTH):
        final = l == DEPTH - 1
        w_bf = w_in[l].astype(BF16)
        nw = norm_w[l].reshape(1, D_MODEL)
        lw = dict(
            conv_w=conv_w[l], conv_b=conv_b[l].reshape(1, D_B),
            wa=_block_diag(lru_wa[l]).astype(BF16), ba=lru_ba[l].reshape(1, D_B),
            wx=_block_diag(lru_wx[l]).astype(BF16), bx=lru_bx[l].reshape(1, D_B),
            lam=lru_lambda[l].reshape(1, D_B),
            ln_w=gmlp_ln_w[l].reshape(1, D_C), ln_b=gmlp_ln_b[l].reshape(1, D_C),
            ws=gmlp_ws[l], bs_mix=jnp.repeat(gmlp_bs[l].T, D_C // G_C, axis=1),
            ws0=jnp.repeat(gmlp_ws[l][:, 0, 0], D_C // G_C).reshape(1, D_C),
            bs0=jnp.repeat(gmlp_bs[l][:, 0], D_C // G_C).reshape(1, D_C),
            w_a=w_br_a[l].astype(BF16), w_b=w_br_b[l].astype(BF16), w_c=w_br_c[l].astype(BF16),
            w_o=w_out[l].astype(BF16), fnw=fnw)

        qT, kT, vT, kb, vTb, kmean, rest = _proj_prompt(l, xp, nw, w_bf, kv_stacks)
        kv_stacks = (kT, vT)
        attn = _attn_prompt(qT, kb, vTb, kmean.reshape(N_BLOCKS, D_A), tpos)
        xp, xbtail, htail = _mix_prompt(final, xp, attn, rest, lw)
        outs["hp"].append(htail[7:8])
        outs["cp"].append(xbtail[5:8].reshape(1, 3, D_B))

        z = _proj_sample(xs, nw, w_bf)
        kn = z[:, D_A:2 * D_A]
        vn = z[:, 2 * D_A:3 * D_A]
        qkv_col = z[:, 0:3 * D_A].reshape(DEC_BATCH, 3 * D_A, 1)
        sel = _select_sample(l, pt_flat, qkv_col, cache_kt2)
        sel_flat = sel[:, :, :, 0].reshape(-1)
        attn_s = _attn_sample(l, pt_flat, sel_flat, qkv_col, cache_kt, cache_vt)
        sc = state_conv[l]
        xs, hnew, cvn = _mix_sample(final, xs, attn_s.reshape(DEC_BATCH, D_A), z,
                                    sc[:, 0], sc[:, 1], sc[:, 2], state_lru_h[l], lw)
        xb_s = z[:, 3 * D_A + R_XB:3 * D_A + R_XB + D_B]
        outs["ks"].append(kn.reshape(DEC_BATCH, 1, N_HEADS, HEAD_DIM))
        outs["vs"].append(vn.reshape(DEC_BATCH, 1, N_HEADS, HEAD_DIM))
        outs["hs"].append(hnew)
        outs["cs"].append(jnp.stack([sc[:, 1], sc[:, 2], xb_s], axis=1))
        outs["cv"].append(cvn.reshape(DEC_BATCH, 1, D_C))

    st = lambda k: jnp.stack(outs[k])
    return (xp.reshape(1, SEQ, D_MODEL), xs.reshape(DEC_BATCH, 1, D_MODEL),
            _heads_last(kv_stacks[0]), _heads_last(kv_stacks[1]), st("ks"), st("vs"), st("hp"), st("cp"), st("hs"), st("cs"), st("cv"))
```

```python
import functools

import jax
import jax.numpy as jnp
import numpy as np
from jax import lax
from jax.experimental import pallas as pl
from jax.experimental.pallas import tpu as pltpu

F32 = jnp.float32
BF16 = jnp.bfloat16
HIGHEST = lax.Precision.HIGHEST

D_MODEL = 1024
SEQ = 16384
DEPTH = 2
DEC_BATCH = 32
PAST_LEN = 16384
PAGE_SIZE = 128
N_HEADS = 8
HEAD_DIM = 64
D_A = N_HEADS * HEAD_DIM
BLOCK_A = 256
N_SEL = 3
D_B = 512
G_B = 8
C_LRU = 8.0
D_C = 512
G_C = 8
CHUNK = 128
D_IN = 4 * D_A + 2 * D_B + 3 * D_C + 3 * D_MODEL
EPS = 1e-6

N_BLOCKS = SEQ // BLOCK_A
N_PAGES = PAST_LEN // PAGE_SIZE
PAGES_PER_BLOCK = BLOCK_A // PAGE_SIZE
N_PAST_BLOCKS = PAST_LEN // BLOCK_A
NEG = -1e30
LOG2E = 1.4426950408889634
ROW_TILE = 256
LANES = 128
HG = 4
SPAGES = 32
STEPS_PER_ROW = N_PAGES // SPAGES
assert (N_HEADS // HG) * N_BLOCKS * SPAGES == DEC_BATCH * N_PAGES
UNROLL = 4
SUM_ROWS = 16
VMEM_LIMIT = 52 * 1024 * 1024

R_GA, R_XB, R_GB, R_U, R_VC, R_GC, R_GM = 0, 512, 1024, 1536, 2048, 2560, 3072
D_REST = D_IN - 3 * D_A


def _gelu(x):
    return 0.5 * x * (1.0 + jnp.tanh(0.7978845608028654 * (x + 0.044715 * (x * x * x))))


def _sigmoid(x):
    return jax.nn.sigmoid(x)


def _silu(x):
    return x * _sigmoid(x)


def _rms_rows(x, w):
    return x * lax.rsqrt(jnp.mean(x * x, axis=-1, keepdims=True) + EPS) * w


def _proj_kernel(n_prev, x_ref, nw_ref, w_ref, *refs):
    qT_ref, kT_ref, vT_ref, kb_ref, vTb_ref, kmean_ref, rest_ref = refs[n_prev:]
    h = _rms_rows(x_ref[...], nw_ref[...]).astype(BF16)

    def proj(a, b):
        return jnp.dot(h, w_ref[:, a:b], preferred_element_type=F32)

    qT_ref[...] = proj(0, D_A).T
    k = proj(D_A, 2 * D_A)
    kT_ref[...] = k.T
    kb_ref[0] = k.astype(BF16)
    kmean_ref[0] = jnp.mean(k, axis=0, keepdims=True)
    vT = proj(2 * D_A, 3 * D_A).T
    vT_ref[...] = vT
    vTb_ref[0] = vT.astype(BF16)
    o = 3 * D_A
    rest_ref[:, R_GA:R_U] = proj(o + R_GA, o + R_U)
    rest_ref[:, R_U:R_GC] = _gelu(proj(o + R_U, o + R_GC))
    rest_ref[:, R_GC:D_REST] = proj(o + R_GC, o + D_REST)


def _proj_prompt(layer, x, nw, w_bf, kv_stacks):
    n = SEQ // ROW_TILE
    const = lambda i: (0, 0)
    colblk = pl.BlockSpec((D_A, ROW_TILE), lambda i: (0, i))
    stackblk = pl.BlockSpec((None, D_A, ROW_TILE), lambda i: (layer, 0, i))
    stack_shape = jax.ShapeDtypeStruct((DEPTH, D_A, SEQ), F32)
    n_prev = len(kv_stacks)
    return pl.pallas_call(
        functools.partial(_proj_kernel, n_prev),
        grid=(n,),
        in_specs=[
            pl.BlockSpec((ROW_TILE, D_MODEL), lambda i: (i, 0)),
            pl.BlockSpec((1, D_MODEL), const),
            pl.BlockSpec((D_MODEL, D_IN), const, pipeline_mode=pl.Buffered(1)),
        ] + [pl.BlockSpec(memory_space=pl.ANY)] * n_prev,
        input_output_aliases={3 + i: 1 + i for i in range(n_prev)},
        out_specs=[
            colblk, stackblk, stackblk,
            pl.BlockSpec((1, ROW_TILE, D_A), lambda i: (i, 0, 0)),
            pl.BlockSpec((1, D_A, ROW_TILE), lambda i: (i, 0, 0)),
            pl.BlockSpec((1, 1, D_A), lambda i: (i, 0, 0)),
            pl.BlockSpec((ROW_TILE, D_REST), lambda i: (i, 0)),
        ],
        out_shape=[
            jax.ShapeDtypeStruct((D_A, SEQ), F32),
            stack_shape,
            stack_shape,
            jax.ShapeDtypeStruct((n, ROW_TILE, D_A), BF16),
            jax.ShapeDtypeStruct((n, D_A, ROW_TILE), BF16),
            jax.ShapeDtypeStruct((n, 1, D_A), F32),
            jax.ShapeDtypeStruct((SEQ, D_REST), F32),
        ],
        compiler_params=pltpu.CompilerParams(
            dimension_semantics=("arbitrary",), vmem_limit_bytes=VMEM_LIMIT),
        name="proj_prompt",
    )(x, nw, w_bf, *kv_stacks)


def _bf16_terms(c, n=3):
    terms, rest = [], float(c)
    for _ in range(n):
        t = float(np.asarray(rest, dtype=np.float32).astype(BF16).astype(np.float32))
        terms.append(t)
        rest -= t
    return terms


def _attn_kernel(pt_ref, qT_ref, kb_ref, vT_ref, kmean_ref, tpos_ref, qs_ref, *refs):
    pages = refs[:SPAGES]
    o_ref, gs_ref = refs[SPAGES:SPAGES + 2]
    qpad_ref, selb_ref, m_ref, acc_ref, sa_ref, sb_ref, p_ref = refs[SPAGES + 2:]
    s_refs = (sa_ref, sb_ref)
    hg = pl.program_id(0)
    own = pl.program_id(1)

    part = (hg * N_BLOCKS + own) % STEPS_PER_ROW
    nb = SPAGES // PAGES_PER_BLOCK
    lane = lax.broadcasted_iota(jnp.int32, (N_HEADS, LANES), 1)

    @pl.when(part == 0)
    def _():
        gs_ref[0] = jnp.zeros((N_HEADS, LANES), F32)

    q_col = qs_ref[0]
    gt = gs_ref[0]
    for i in range(nb):
        both = pages[PAGES_PER_BLOCK * i][...] + pages[PAGES_PER_BLOCK * i + 1][...]
        per_head = jnp.sum((both * q_col).reshape(N_HEADS, HEAD_DIM, PAGE_SIZE), axis=1)
        g_col = jnp.sum(per_head, axis=1, keepdims=True) * (1.0 / BLOCK_A)
        gt = jnp.where(lane == part * nb + i, g_col, gt)
    gs_ref[0] = gt

    first = hg == 0
    qT = qT_ref[...]
    blk = lax.broadcasted_iota(jnp.int32, (N_BLOCKS, BLOCK_A), 0)
    rid = lax.broadcasted_iota(jnp.int32, (2 * HEAD_DIM, BLOCK_A), 0)
    half = rid // HEAD_DIM

    def slope(h):
        return jnp.where(first, 2.0 ** -(h + 1), 2.0 ** -(HG + h + 1))

    for h in range(HG):
        hp, par = divmod(h, 2)
        rows = qT[LANES * hp:LANES * (hp + 1), :]
        qpad = jnp.where(half == par, rows, 0.0)
        g = jnp.dot(kmean_ref[:, LANES * hp:LANES * (hp + 1)], qpad,
                    precision=HIGHEST, preferred_element_type=F32)
        g = jnp.where(blk < own, g, -jnp.inf)
        selb = jnp.full((N_BLOCKS, BLOCK_A), NEG, F32)
        for r in range(N_SEL):
            mx = jnp.max(g, axis=0, keepdims=True)
            idx = jnp.min(jnp.where(g == mx, blk, N_BLOCKS), axis=0, keepdims=True)
            hit = blk == idx
            selb = jnp.where(jnp.logical_and(hit, r < own), 0.0, selb)
            g = jnp.where(hit, -jnp.inf, g)
        selb_ref[h] = selb
        qpad_ref[h, 0:LANES, :] = (qpad * (HEAD_DIM ** -0.5 * LOG2E)).astype(BF16)
        ca = _bf16_terms(2.0 ** -(h + 1) * LOG2E)
        cb = _bf16_terms(2.0 ** -(HG + h + 1) * LOG2E)
        c0, c1, c2 = [jnp.where(first, x, y) for x, y in zip(ca, cb)]
        qpad_ref[h, LANES:2 * LANES, :] = jnp.where(
            rid == 0, c0, jnp.where(rid == 1, c1, jnp.where(rid == 2, c2, 0.0))).astype(BF16)

    t_io = lax.broadcasted_iota(jnp.int32, (BLOCK_A, BLOCK_A), 0)
    u_io = lax.broadcasted_iota(jnp.int32, (BLOCK_A, BLOCK_A), 1)

    ones = jnp.ones((SUM_ROWS, BLOCK_A), BF16)

    def scores_head(j, slot, h, is_own=False):
        hp = h // 2
        k_aug = jnp.concatenate([kb_ref[j, :, LANES * hp:LANES * (hp + 1)], tpos_ref[...]], axis=1)
        s = jnp.dot(k_aug, qpad_ref[h], preferred_element_type=F32)
        if is_own:
            s = jnp.where(t_io <= u_io, s, NEG)
        s_refs[slot][h] = s

    def softmax_head(j, slot, h, valid=None):
        blk_max = jnp.max(s_refs[slot][h], axis=0, keepdims=True)
        if valid is None:
            alpha, m_new, m_eff = None, blk_max, blk_max
        else:
            off = ((j - own) * BLOCK_A).astype(F32)
            rb = (selb_ref[h, pl.ds(j, 1), :] + slope(h) * off) * LOG2E
            rb = jnp.where(valid, rb, NEG)
            m_old = m_ref[h]
            m_new = jnp.maximum(m_old, blk_max + rb)
            alpha = jnp.exp2(m_old - m_new)
            m_eff = m_new - rb
        m_ref[h] = m_new
        p_ref[h] = jnp.exp2(s_refs[slot][h] - m_eff).astype(BF16)
        return alpha

    def pv_head(j, h, alpha):
        v_aug = jnp.concatenate([vT_ref[j, HEAD_DIM * h:HEAD_DIM * (h + 1), :], ones], axis=0)
        pv = jnp.dot(v_aug, p_ref[h], preferred_element_type=F32)
        acc_ref[h] = pv if alpha is None else alpha * acc_ref[h] + pv

    def step(j_next, slot_next, j, slot, valid):
        alphas = [None] * HG
        for h in range(HG):
            scores_head(j_next, slot_next, h)
            alphas[h] = softmax_head(j, slot, h, valid)
            if h > 0:
                pv_head(j, h - 1, alphas[h - 1])
        pv_head(j, HG - 1, alphas[HG - 1])

    for h in range(HG):
        scores_head(own, 0, h, is_own=True)
    for h in range(HG):
        scores_head(0, 1, h)
        softmax_head(own, 0, h)
        pv_head(own, h, None)

    def steps(i, carry):
        for u in range(UNROLL):
            j = UNROLL * i + u
            step(jnp.minimum(j + 1, N_BLOCKS - 1), u % 2,
                 jnp.minimum(j, N_BLOCKS - 1), (u + 1) % 2, j < own)
        return carry

    lax.fori_loop(0, (own + UNROLL - 1) // UNROLL, steps, 0)

    outT = jnp.concatenate(
        [acc_ref[h, 0:HEAD_DIM, :] / acc_ref[h, HEAD_DIM:HEAD_DIM + 1, :] for h in range(HG)],
        axis=0)
    o_ref[...] = outT.T


def _attn_prompt(layer, pt_flat, qT, kb, vT, kmean, tpos, qkv_col, cache_kt2):
    gw = HG * HEAD_DIM
    row_of = lambda g, m: (g * N_BLOCKS + m) // STEPS_PER_ROW

    def page_spec(i):
        return pl.BlockSpec(
            (None, None, D_A, PAGE_SIZE),
            lambda g, m, pt: (layer, pt[(g * N_BLOCKS + m) * SPAGES + i], 0, 0))

    grid_spec = pltpu.PrefetchScalarGridSpec(
        num_scalar_prefetch=1,
        grid=(N_HEADS // HG, N_BLOCKS),
        in_specs=[
            pl.BlockSpec((gw, BLOCK_A), lambda g, m, pt: (g, m)),
            pl.BlockSpec((N_BLOCKS, BLOCK_A, gw), lambda g, m, pt: (0, 0, g),
                         pipeline_mode=pl.Buffered(1)),
            pl.BlockSpec((N_BLOCKS, gw, BLOCK_A), lambda g, m, pt: (0, g, 0),
                         pipeline_mode=pl.Buffered(1)),
            pl.BlockSpec((N_BLOCKS, gw), lambda g, m, pt: (0, g)),
            pl.BlockSpec((BLOCK_A, LANES), lambda g, m, pt: (0, 0)),
            pl.BlockSpec((1, D_A, 1), lambda g, m, pt: (row_of(g, m), 0, 0)),
        ] + [page_spec(i) for i in range(SPAGES)],
        out_specs=[
            pl.BlockSpec((BLOCK_A, gw), lambda g, m, pt: (m, g)),
            pl.BlockSpec((1, N_HEADS, LANES), lambda g, m, pt: (row_of(g, m), 0, 0)),
        ],
        scratch_shapes=[
            pltpu.VMEM((HG, 2 * LANES, BLOCK_A), BF16),
            pltpu.VMEM((HG, N_BLOCKS, BLOCK_A), F32),
            pltpu.VMEM((HG, 1, BLOCK_A), F32),
            pltpu.VMEM((HG, HEAD_DIM + SUM_ROWS, BLOCK_A), F32),
            pltpu.VMEM((HG, BLOCK_A, BLOCK_A), F32),
            pltpu.VMEM((HG, BLOCK_A, BLOCK_A), F32),
            pltpu.VMEM((HG, BLOCK_A, BLOCK_A), BF16),
        ],
    )
    return pl.pallas_call(
        _attn_kernel,
        grid_spec=grid_spec,
        out_shape=[
            jax.ShapeDtypeStruct((SEQ, D_A), F32),
            jax.ShapeDtypeStruct((DEC_BATCH, N_HEADS, LANES), F32),
        ],
        compiler_params=pltpu.CompilerParams(
            dimension_semantics=("arbitrary", "arbitrary"), vmem_limit_bytes=VMEM_LIMIT),
        name="attn_prompt",
    )(pt_flat, qT, kb, vT, kmean, tpos, qkv_col, *([cache_kt2] * SPAGES))


def _lru_gates(xc, wa, ba, wx, bx, lam):
    xcb = xc.astype(BF16)
    r = _sigmoid(jnp.dot(xcb, wa, preferred_element_type=F32) + ba)
    ig = _sigmoid(jnp.dot(xcb, wx, preferred_element_type=F32) + bx)
    nl = -lam
    softplus = jnp.maximum(nl, 0.0) + jnp.log1p(jnp.exp(-jnp.abs(nl)))
    log_a = (-C_LRU * softplus) * r
    a = jnp.exp(log_a)
    b = jnp.sqrt(-jnp.tanh(log_a) * (a * a + 1.0)) * (ig * xc)
    return a, b


def _layernorm(v, w, b):
    mu = jnp.mean(v, axis=-1, keepdims=True)
    d = v - mu
    var = jnp.mean(d * d, axis=-1, keepdims=True)
    return d * lax.rsqrt(var + EPS) * w + b


def _merge(x, a, b, c, ga, gb, gc, gm, wba, wbb, wbc, wo):
    pa = jnp.dot((a * _silu(ga)).astype(BF16), wba, preferred_element_type=F32)
    pb = jnp.dot((b * _silu(gb)).astype(BF16), wbb, preferred_element_type=F32)
    pc = jnp.dot((c * _silu(gc)).astype(BF16), wbc, preferred_element_type=F32)
    m = (_sigmoid(gm[:, 0:D_MODEL]) * pa + _sigmoid(gm[:, D_MODEL:2 * D_MODEL]) * pb
         + _sigmoid(gm[:, 2 * D_MODEL:3 * D_MODEL]) * pc)
    return x + jnp.dot(m.astype(BF16), wo, preferred_element_type=F32)


def _mix_kernel(final, x_ref, attn_ref, rest_ref, convw_ref, convb_ref, wa_ref, ba_ref, wx_ref,
                bx_ref, lam_ref, lnw_ref, lnb_ref, ws_ref, bsm_ref, wba_ref, wbb_ref, wbc_ref,
                wo_ref, fnw_ref, out_ref, xbtail_ref, htail_ref, ext_ref, hc_ref):
    i = pl.program_id(0)
    T = ROW_TILE

    @pl.when(i == 0)
    def _():
        ext_ref[0:8, :] = jnp.zeros((8, D_B), F32)
        hc_ref[...] = jnp.zeros((1, D_B), F32)

    xb = rest_ref[:, R_XB:R_XB + D_B]
    ext_ref[8:8 + T, :] = xb
    cw = convw_ref[...]
    xc = (convb_ref[...] + ext_ref[5:5 + T, :] * cw[0:1] + ext_ref[6:6 + T, :] * cw[1:2]
          + ext_ref[7:7 + T, :] * cw[2:3] + xb * cw[3:4])
    ext_ref[0:8, :] = xb[T - 8:T, :]
    xbtail_ref[...] = xb[T - 8:T, :]

    a, b = _lru_gates(xc, wa_ref[...], ba_ref[...], wx_ref[...], bx_ref[...], lam_ref[...])
    row = lax.broadcasted_iota(jnp.int32, (T, D_B), 0)
    d = 1
    while d < T:
        a_sh = jnp.where(row >= d, pltpu.roll(a, d, 0), 1.0)
        b_sh = jnp.where(row >= d, pltpu.roll(b, d, 0), 0.0)
        b = a * b_sh + b
        a = a * a_sh
        d *= 2
    hseq = b + a * hc_ref[...]
    hc_ref[...] = hseq[T - 1:T, :]
    htail_ref[...] = hseq[T - 8:T, :]

    vn = _layernorm(rest_ref[:, R_VC:R_VC + D_C], lnw_ref[...], lnb_ref[...]).astype(BF16)
    lane_g = lax.broadcasted_iota(jnp.int32, (CHUNK, D_C), 1) // (D_C // G_C)
    tri = (lax.broadcasted_iota(jnp.int32, (CHUNK, CHUNK), 0)
           >= lax.broadcasted_iota(jnp.int32, (CHUNK, CHUNK), 1))
    mixes = []
    for c in range(T // CHUNK):
        vch = vn[c * CHUNK:(c + 1) * CHUNK, :]
        mix = jnp.zeros((CHUNK, D_C), F32)
        for g in range(G_C):
            wg = jnp.where(tri, ws_ref[g], 0.0).astype(BF16)
            mix = jnp.where(lane_g == g, jnp.dot(wg, vch, preferred_element_type=F32), mix)
        mixes.append(mix + bsm_ref[...])
    cm = rest_ref[:, R_U:R_U + D_C] * jnp.concatenate(mixes, axis=0)

    xn = _merge(x_ref[...], attn_ref[...], hseq, cm,
                rest_ref[:, R_GA:R_GA + D_A], rest_ref[:, R_GB:R_GB + D_B],
                rest_ref[:, R_GC:R_GC + D_C], rest_ref[:, R_GM:R_GM + 3 * D_MODEL],
                wba_ref[...], wbb_ref[...], wbc_ref[...], wo_ref[...])
    out_ref[...] = _rms_rows(xn, fnw_ref[...]) if final else xn


def _mix_prompt(final, x, attn, rest, lw):
    n = SEQ // ROW_TILE
    row = lambda i: (i, 0)
    c2 = lambda i: (0, 0)
    c3 = lambda i: (0, 0, 0)
    vec = lambda d: pl.BlockSpec((1, d), c2)
    return pl.pallas_call(
        functools.partial(_mix_kernel, final),
        grid=(n,),
        in_specs=[
            pl.BlockSpec((ROW_TILE, D_MODEL), row),
            pl.BlockSpec((ROW_TILE, D_A), row),
            pl.BlockSpec((ROW_TILE, D_REST), row),
            pl.BlockSpec((4, D_B), c2), vec(D_B),
            pl.BlockSpec((D_B, D_B), c2), vec(D_B),
            pl.BlockSpec((D_B, D_B), c2), vec(D_B), vec(D_B),
            vec(D_C), vec(D_C),
            pl.BlockSpec((G_C, CHUNK, CHUNK), c3),
            pl.BlockSpec((CHUNK, D_C), c2),
            pl.BlockSpec((D_A, D_MODEL), c2),
            pl.BlockSpec((D_B, D_MODEL), c2),
            pl.BlockSpec((D_C, D_MODEL), c2),
            pl.BlockSpec((D_MODEL, D_MODEL), c2),
            vec(D_MODEL),
        ],
        out_specs=[
            pl.BlockSpec((ROW_TILE, D_MODEL), row),
            pl.BlockSpec((8, D_B), c2),
            pl.BlockSpec((8, D_B), c2),
        ],
        out_shape=[
            jax.ShapeDtypeStruct((SEQ, D_MODEL), F32),
            jax.ShapeDtypeStruct((8, D_B), F32),
            jax.ShapeDtypeStruct((8, D_B), F32),
        ],
        scratch_shapes=[
            pltpu.VMEM((8 + ROW_TILE, D_B), F32),
            pltpu.VMEM((1, D_B), F32),
        ],
        compiler_params=pltpu.CompilerParams(
            dimension_semantics=("arbitrary",), vmem_limit_bytes=VMEM_LIMIT),
        name="mix_prompt",
    )(x, attn, rest, lw["conv_w"], lw["conv_b"], lw["wa"], lw["ba"], lw["wx"], lw["bx"],
      lw["lam"], lw["ln_w"], lw["ln_b"], lw["ws"], lw["bs_mix"], lw["w_a"], lw["w_b"], lw["w_c"],
      lw["w_o"], lw["fnw"])


S_COLS = 1536


def _sproj_kernel(x_ref, nw_ref, w_ref, z_ref):
    h = _rms_rows(x_ref[...], nw_ref[...]).astype(BF16)
    z_ref[...] = jnp.dot(h, w_ref[...], preferred_element_type=F32)


def _proj_sample(x, nw, w_bf):
    return pl.pallas_call(
        _sproj_kernel,
        grid=(D_IN // S_COLS,),
        in_specs=[
            pl.BlockSpec((DEC_BATCH, D_MODEL), lambda j: (0, 0)),
            pl.BlockSpec((1, D_MODEL), lambda j: (0, 0)),
            pl.BlockSpec((D_MODEL, S_COLS), lambda j: (0, j)),
        ],
        out_specs=pl.BlockSpec((DEC_BATCH, S_COLS), lambda j: (0, j)),
        out_shape=jax.ShapeDtypeStruct((DEC_BATCH, D_IN), F32),
        compiler_params=pltpu.CompilerParams(dimension_semantics=("arbitrary",)),
        name="proj_sample",
    )(x, nw, w_bf)


def _sselect_kernel(g_ref, sel_ref):
    blk = lax.broadcasted_iota(jnp.int32, (N_HEADS, LANES), 1)
    g = jnp.where(blk < N_PAST_BLOCKS, g_ref[0], -jnp.inf)
    for r in range(N_SEL):
        mx = jnp.max(g, axis=1, keepdims=True)
        idx = jnp.min(jnp.where(g == mx, blk, LANES), axis=1, keepdims=True)
        sel_ref[0, r] = jnp.broadcast_to(idx, (N_HEADS, LANES))
        g = jnp.where(blk == idx, -jnp.inf, g)


def _select_sample(gate_s):
    return pl.pallas_call(
        _sselect_kernel,
        grid=(DEC_BATCH,),
        in_specs=[pl.BlockSpec((1, N_HEADS, LANES), lambda b: (b, 0, 0))],
        out_specs=pl.BlockSpec((1, N_SEL, N_HEADS, LANES), lambda b: (b, 0, 0, 0)),
        out_shape=jax.ShapeDtypeStruct((DEC_BATCH, N_SEL, N_HEADS, LANES), jnp.int32),
        compiler_params=pltpu.CompilerParams(dimension_semantics=("arbitrary",)),
        name="select_sample",
    )(gate_s)


N_SEL_PAGES = N_SEL * PAGES_PER_BLOCK


def _sattn_kernel(pt_ref, sel_ref, q_ref, kn_ref, vn_ref, *refs):
    kp = refs[:N_SEL_PAGES]
    vp = refs[N_SEL_PAGES:2 * N_SEL_PAGES]
    o_ref = refs[2 * N_SEL_PAGES]
    b = pl.program_id(0)
    h = pl.program_id(1)
    q = q_ref[0] * (HEAD_DIM ** -0.5)
    slope = jnp.exp2(-(jnp.full((1, 1), h, jnp.int32) + 1).astype(F32))
    t = lax.broadcasted_iota(jnp.int32, (1, PAGE_SIZE), 1).astype(F32)

    s_list = []
    for r in range(N_SEL):
        blk = sel_ref[(b * N_SEL + r) * N_HEADS + h]
        for hf in range(PAGES_PER_BLOCK):
            s = jnp.sum(kp[r * PAGES_PER_BLOCK + hf][...] * q, axis=0, keepdims=True)
            start = (PAST_LEN - blk * BLOCK_A - hf * PAGE_SIZE).astype(F32)
            s_list.append(s - slope * (start - t))
    s_own = jnp.sum(kn_ref[0] * q, axis=0, keepdims=True)

    m = s_own
    for s in s_list:
        m = jnp.maximum(m, jnp.max(s, axis=1, keepdims=True))
    p_own = jnp.exp(s_own - m)
    l = p_own
    acc = p_own * vn_ref[0]
    for i, s in enumerate(s_list):
        p = jnp.exp(s - m)
        l = l + jnp.sum(p, axis=1, keepdims=True)
        acc = acc + jnp.sum(vp[i][...] * p, axis=1, keepdims=True)
    o_ref[0, 0] = acc / l


def _attn_sample(layer, pt_flat, sel_flat, qkv_col, cache_kt, cache_vt):
    def page_spec(r, hf):
        return pl.BlockSpec(
            (None, None, None, HEAD_DIM, PAGE_SIZE),
            lambda b, h, pt, sel: (
                layer,
                pt[b * N_PAGES + sel[(b * N_SEL + r) * N_HEADS + h] * PAGES_PER_BLOCK + hf],
                h, 0, 0))

    def col(part):
        return pl.BlockSpec((1, HEAD_DIM, 1), lambda b, h, pt, sel: (b, part * N_HEADS + h, 0))

    pages = [page_spec(r, hf) for r in range(N_SEL) for hf in range(PAGES_PER_BLOCK)]
    grid_spec = pltpu.PrefetchScalarGridSpec(
        num_scalar_prefetch=2,
        grid=(DEC_BATCH, N_HEADS),
        in_specs=[col(0), col(1), col(2)] + pages + pages,
        out_specs=pl.BlockSpec((1, 1, HEAD_DIM, 1), lambda b, h, pt, sel: (b, h, 0, 0)),
    )
    return pl.pallas_call(
        _sattn_kernel,
        grid_spec=grid_spec,
        out_shape=jax.ShapeDtypeStruct((DEC_BATCH, N_HEADS, HEAD_DIM, 1), F32),
        compiler_params=pltpu.CompilerParams(dimension_semantics=("arbitrary", "arbitrary")),
        name="attn_sample",
    )(pt_flat, sel_flat, qkv_col, qkv_col, qkv_col,
      *([cache_kt] * N_SEL_PAGES), *([cache_vt] * N_SEL_PAGES))


def _smix_kernel(final, x_ref, attn_ref, z_ref, sc0_ref, sc1_ref, sc2_ref, h0_ref, convw_ref,
                 convb_ref, wa_ref, ba_ref, wx_ref, bx_ref, lam_ref, lnw_ref, lnb_ref, ws0_ref,
                 bs0_ref, wba_ref, wbb_ref, wbc_ref, wo_ref, fnw_ref, out_ref, hnew_ref, vn_ref):
    o = 3 * D_A
    xb = z_ref[:, o + R_XB:o + R_XB + D_B]
    cw = convw_ref[...]
    xc = (convb_ref[...] + sc0_ref[...] * cw[0:1] + sc1_ref[...] * cw[1:2]
          + sc2_ref[...] * cw[2:3] + xb * cw[3:4])
    a, b = _lru_gates(xc, wa_ref[...], ba_ref[...], wx_ref[...], bx_ref[...], lam_ref[...])
    hnew = b + a * h0_ref[...]
    hnew_ref[...] = hnew

    vn = _layernorm(_gelu(z_ref[:, o + R_VC:o + R_VC + D_C]), lnw_ref[...], lnb_ref[...])
    vn_ref[...] = vn
    cm = _gelu(z_ref[:, o + R_U:o + R_U + D_C]) * (ws0_ref[...] * vn + bs0_ref[...])

    xn = _merge(x_ref[...], attn_ref[...], hnew, cm,
                z_ref[:, o + R_GA:o + R_GA + D_A], z_ref[:, o + R_GB:o + R_GB + D_B],
                z_ref[:, o + R_GC:o + R_GC + D_C], z_ref[:, o + R_GM:o + R_GM + 3 * D_MODEL],
                wba_ref[...], wbb_ref[...], wbc_ref[...], wo_ref[...])
    out_ref[...] = _rms_rows(xn, fnw_ref[...]) if final else xn


def _mix_sample(final, x, attn, z, sc0, sc1, sc2, h0, lw):
    return pl.pallas_call(
        functools.partial(_smix_kernel, final),
        out_shape=[
            jax.ShapeDtypeStruct((DEC_BATCH, D_MODEL), F32),
            jax.ShapeDtypeStruct((DEC_BATCH, D_B), F32),
            jax.ShapeDtypeStruct((DEC_BATCH, D_C), F32),
        ],
        compiler_params=pltpu.CompilerParams(vmem_limit_bytes=VMEM_LIMIT),
        name="mix_sample",
    )(x, attn, z, sc0, sc1, sc2, h0, lw["conv_w"], lw["conv_b"], lw["wa"], lw["ba"], lw["wx"],
      lw["bx"], lw["lam"], lw["ln_w"], lw["ln_b"], lw["ws0"], lw["bs0"], lw["w_a"], lw["w_b"],
      lw["w_c"], lw["w_o"], lw["fnw"])


def _block_diag(w):
    g, n, _ = w.shape
    eye = jnp.eye(g, dtype=w.dtype)
    return (w[:, :, None, :] * eye[:, None, :, None]).reshape(g * n, g * n)


def _heads_last(xT):
    return xT.reshape(DEPTH, N_HEADS, HEAD_DIM, -1).transpose(0, 3, 1, 2)[:, None]


def kernel(x_prompt, x_sample, cache_k, cache_v, page_table, state_lru_h, state_conv, norm_w, w_in, w_br_a, w_br_b, w_br_c, w_out, conv_w, conv_b, lru_wa, lru_ba, lru_wx, lru_bx, lru_lambda, gmlp_ln_w, gmlp_ln_b, gmlp_ws, gmlp_bs, final_norm_w):
    n_pool = cache_k.shape[1]
    cache_kt = cache_k.transpose(0, 1, 3, 4, 2)
    cache_vt = cache_v.transpose(0, 1, 3, 4, 2)
    cache_kt2 = cache_kt.reshape(DEPTH, n_pool, D_A, PAGE_SIZE)
    pt_flat = page_table.reshape(-1)
    tpos = jnp.where(jnp.arange(LANES)[None, :] < 3, jnp.arange(BLOCK_A, dtype=F32)[:, None], 0.0).astype(BF16)
    fnw = final_norm_w.reshape(1, D_MODEL)

    xp = x_prompt.reshape(SEQ, D_MODEL)
    xs = x_sample.reshape(DEC_BATCH, D_MODEL)
    outs = {k: [] for k in ("ks", "vs", "hp", "cp", "hs", "cs", "cv")}
    kv_stacks = ()
    for l in range(DEPTH):
        final = l == DEPTH - 1
        w_bf = w_in[l].astype(BF16)
        nw = norm_w[l].reshape(1, D_MODEL)
        lw = dict(
            conv_w=conv_w[l], conv_b=conv_b[l].reshape(1, D_B),
            wa=_block_diag(lru_wa[l]).astype(BF16), ba=lru_ba[l].reshape(1, D_B),
            wx=_block_diag(lru_wx[l]).astype(BF16), bx=lru_bx[l].reshape(1, D_B),
            lam=lru_lambda[l].reshape(1, D_B),
            ln_w=gmlp_ln_w[l].reshape(1, D_C), ln_b=gmlp_ln_b[l].reshape(1, D_C),
            ws=gmlp_ws[l], bs_mix=jnp.repeat(gmlp_bs[l].T, D_C // G_C, axis=1),
            ws0=jnp.repeat(gmlp_ws[l][:, 0, 0], D_C // G_C).reshape(1, D_C),
            bs0=jnp.repeat(gmlp_bs[l][:, 0], D_C // G_C).reshape(1, D_C),
            w_a=w_br_a[l].astype(BF16), w_b=w_br_b[l].astype(BF16), w_c=w_br_c[l].astype(BF16),
            w_o=w_out[l].astype(BF16), fnw=fnw)

        z = _proj_sample(xs, nw, w_bf)
        kn = z[:, D_A:2 * D_A]
        vn = z[:, 2 * D_A:3 * D_A]
        qkv_col = z[:, 0:3 * D_A].reshape(DEC_BATCH, 3 * D_A, 1)

        qT, kT, vT, kb, vTb, kmean, rest = _proj_prompt(l, xp, nw, w_bf, kv_stacks)
        kv_stacks = (kT, vT)
        attn, gate_s = _attn_prompt(l, pt_flat, qT, kb, vTb, kmean.reshape(N_BLOCKS, D_A), tpos,
                                    qkv_col, cache_kt2)
        xp, xbtail, htail = _mix_prompt(final, xp, attn, rest, lw)
        outs["hp"].append(htail[7:8])
        outs["cp"].append(xbtail[5:8].reshape(1, 3, D_B))

        sel = _select_sample(gate_s)
        sel_flat = sel[:, :, :, 0].reshape(-1)
        attn_s = _attn_sample(l, pt_flat, sel_flat, qkv_col, cache_kt, cache_vt)
        sc = state_conv[l]
        xs, hnew, cvn = _mix_sample(final, xs, attn_s.reshape(DEC_BATCH, D_A), z,
                                    sc[:, 0], sc[:, 1], sc[:, 2], state_lru_h[l], lw)
        xb_s = z[:, 3 * D_A + R_XB:3 * D_A + R_XB + D_B]
        outs["ks"].append(kn.reshape(DEC_BATCH, 1, N_HEADS, HEAD_DIM))
        outs["vs"].append(vn.reshape(DEC_BATCH, 1, N_HEADS, HEAD_DIM))
        outs["hs"].append(hnew)
        outs["cs"].append(jnp.stack([sc[:, 1], sc[:, 2], xb_s], axis=1))
        outs["cv"].append(cvn.reshape(DEC_BATCH, 1, D_C))

    st = lambda k: jnp.stack(outs[k])
    return (xp.reshape(1, SEQ, D_MODEL), xs.reshape(DEC_BATCH, 1, D_MODEL),
            _heads_last(kv_stacks[0]), _heads_last(kv_stacks[1]), st("ks"), st("vs"), st("hp"), st("cp"), st("hs"), st("cs"), st("cv"))
```

```python
import functools

import jax
import jax.numpy as jnp
import numpy as np
from jax import lax
from jax.experimental import pallas as pl
from jax.experimental.pallas import tpu as pltpu

F32 = jnp.float32
BF16 = jnp.bfloat16
HIGHEST = lax.Precision.HIGHEST

D_MODEL = 1024
SEQ = 16384
DEPTH = 2
DEC_BATCH = 32
PAST_LEN = 16384
PAGE_SIZE = 128
N_HEADS = 8
HEAD_DIM = 64
D_A = N_HEADS * HEAD_DIM
BLOCK_A = 256
N_SEL = 3
D_B = 512
G_B = 8
C_LRU = 8.0
D_C = 512
G_C = 8
CHUNK = 128
D_IN = 4 * D_A + 2 * D_B + 3 * D_C + 3 * D_MODEL
EPS = 1e-6

N_BLOCKS = SEQ // BLOCK_A
N_PAGES = PAST_LEN // PAGE_SIZE
PAGES_PER_BLOCK = BLOCK_A // PAGE_SIZE
N_PAST_BLOCKS = PAST_LEN // BLOCK_A
NEG = -1e30
LOG2E = 1.4426950408889634
ROW_TILE = 256
LANES = 128
HG = 4
SPAGES = 32
STEPS_PER_ROW = N_PAGES // SPAGES
assert (N_HEADS // HG) * N_BLOCKS * SPAGES == DEC_BATCH * N_PAGES
UNROLL = 4
SUM_ROWS = 16
VMEM_LIMIT = 52 * 1024 * 1024

R_GA, R_XB, R_GB, R_U, R_VC, R_GC, R_GM = 0, 512, 1024, 1536, 2048, 2560, 3072
D_REST = D_IN - 3 * D_A


def _gelu(x):
    return 0.5 * x * (1.0 + jnp.tanh(0.7978845608028654 * (x + 0.044715 * (x * x * x))))


def _sigmoid(x):
    return jax.nn.sigmoid(x)


def _silu(x):
    return x * _sigmoid(x)


def _rms_rows(x, w):
    return x * lax.rsqrt(jnp.mean(x * x, axis=-1, keepdims=True) + EPS) * w


def _proj_kernel(n_prev, x_ref, nw_ref, w_ref, *refs):
    qT_ref, kT_ref, vT_ref, kb_ref, vTb_ref, kmean_ref, rest_ref = refs[n_prev:]
    h = _rms_rows(x_ref[...], nw_ref[...]).astype(BF16)

    def proj(a, b):
        return jnp.dot(h, w_ref[:, a:b], preferred_element_type=F32)

    qT_ref[...] = proj(0, D_A).T
    k = proj(D_A, 2 * D_A)
    kT_ref[...] = k.T
    kb_ref[0] = k.astype(BF16)
    kmean_ref[0] = jnp.mean(k, axis=0, keepdims=True)
    vT = proj(2 * D_A, 3 * D_A).T
    vT_ref[...] = vT
    vTb_ref[0] = vT.astype(BF16)
    o = 3 * D_A
    rest_ref[:, R_GA:R_U] = proj(o + R_GA, o + R_U)
    rest_ref[:, R_U:R_GC] = _gelu(proj(o + R_U, o + R_GC))
    rest_ref[:, R_GC:D_REST] = proj(o + R_GC, o + D_REST)


def _proj_prompt(layer, x, nw, w_bf, kv_stacks):
    n = SEQ // ROW_TILE
    const = lambda i: (0, 0)
    colblk = pl.BlockSpec((D_A, ROW_TILE), lambda i: (0, i))
    stackblk = pl.BlockSpec((None, D_A, ROW_TILE), lambda i: (layer, 0, i))
    stack_shape = jax.ShapeDtypeStruct((DEPTH, D_A, SEQ), F32)
    n_prev = len(kv_stacks)
    return pl.pallas_call(
        functools.partial(_proj_kernel, n_prev),
        grid=(n,),
        in_specs=[
            pl.BlockSpec((ROW_TILE, D_MODEL), lambda i: (i, 0)),
            pl.BlockSpec((1, D_MODEL), const),
            pl.BlockSpec((D_MODEL, D_IN), const, pipeline_mode=pl.Buffered(1)),
        ] + [pl.BlockSpec(memory_space=pl.ANY)] * n_prev,
        input_output_aliases={3 + i: 1 + i for i in range(n_prev)},
        out_specs=[
            colblk, stackblk, stackblk,
            pl.BlockSpec((1, ROW_TILE, D_A), lambda i: (i, 0, 0)),
            pl.BlockSpec((1, D_A, ROW_TILE), lambda i: (i, 0, 0)),
            pl.BlockSpec((1, 1, D_A), lambda i: (i, 0, 0)),
            pl.BlockSpec((ROW_TILE, D_REST), lambda i: (i, 0)),
        ],
        out_shape=[
            jax.ShapeDtypeStruct((D_A, SEQ), F32),
            stack_shape,
            stack_shape,
            jax.ShapeDtypeStruct((n, ROW_TILE, D_A), BF16),
            jax.ShapeDtypeStruct((n, D_A, ROW_TILE), BF16),
            jax.ShapeDtypeStruct((n, 1, D_A), F32),
            jax.ShapeDtypeStruct((SEQ, D_REST), F32),
        ],
        compiler_params=pltpu.CompilerParams(
            dimension_semantics=("arbitrary",), vmem_limit_bytes=VMEM_LIMIT),
        name="proj_prompt",
    )(x, nw, w_bf, *kv_stacks)


def _bf16_terms(c, n=3):
    terms, rest = [], float(c)
    for _ in range(n):
        t = float(np.asarray(rest, dtype=np.float32).astype(BF16).astype(np.float32))
        terms.append(t)
        rest -= t
    return terms


def _attn_kernel(pt_ref, qT_ref, kb_ref, vT_ref, kmean_ref, tpos_ref, qs_ref, *refs):
    pages = refs[:SPAGES]
    o_ref, gs_ref = refs[SPAGES:SPAGES + 2]
    qpad_ref, selb_ref, m_ref, acc_ref, sa_ref, sb_ref, p_ref = refs[SPAGES + 2:]
    s_refs = (sa_ref, sb_ref)
    hg = pl.program_id(0)
    own = pl.program_id(1)

    part = (hg * N_BLOCKS + own) % STEPS_PER_ROW
    nb = SPAGES // PAGES_PER_BLOCK
    lane = lax.broadcasted_iota(jnp.int32, (N_HEADS, LANES), 1)

    @pl.when(part == 0)
    def _():
        gs_ref[0] = jnp.zeros((N_HEADS, LANES), F32)

    slab = 2 * HEAD_DIM
    per_head = [[] for _ in range(nb)]
    for sl in range(D_A // slab):
        q_slab = jnp.broadcast_to(qs_ref[0, slab * sl:slab * (sl + 1), :], (slab, PAGE_SIZE))
        for i in range(nb):
            both = (pages[PAGES_PER_BLOCK * i][slab * sl:slab * (sl + 1), :]
                    + pages[PAGES_PER_BLOCK * i + 1][slab * sl:slab * (sl + 1), :])
            per_head[i].append(jnp.sum((both * q_slab).reshape(2, HEAD_DIM, PAGE_SIZE), axis=1))
    gt = gs_ref[0]
    for i in range(nb):
        g_col = jnp.sum(jnp.concatenate(per_head[i], axis=0), axis=1, keepdims=True) * (1.0 / BLOCK_A)
        gt = jnp.where(lane == part * nb + i, g_col, gt)
    gs_ref[0] = gt

    first = hg == 0
    qT = qT_ref[...]
    blk = lax.broadcasted_iota(jnp.int32, (N_BLOCKS, BLOCK_A), 0)
    rid = lax.broadcasted_iota(jnp.int32, (2 * HEAD_DIM, BLOCK_A), 0)
    half = rid // HEAD_DIM

    def slope(h):
        return jnp.where(first, 2.0 ** -(h + 1), 2.0 ** -(HG + h + 1))

    for h in range(HG):
        hp, par = divmod(h, 2)
        rows = qT[LANES * hp:LANES * (hp + 1), :]
        qpad = jnp.where(half == par, rows, 0.0)
        g = jnp.dot(kmean_ref[:, LANES * hp:LANES * (hp + 1)], qpad,
                    precision=HIGHEST, preferred_element_type=F32)
        g = jnp.where(blk < own, g, -jnp.inf)
        selb = jnp.full((N_BLOCKS, BLOCK_A), NEG, F32)
        for r in range(N_SEL):
            mx = jnp.max(g, axis=0, keepdims=True)
            idx = jnp.min(jnp.where(g == mx, blk, N_BLOCKS), axis=0, keepdims=True)
            hit = blk == idx
            selb = jnp.where(jnp.logical_and(hit, r < own), 0.0, selb)
            g = jnp.where(hit, -jnp.inf, g)
        selb_ref[h] = selb
        qpad_ref[h, 0:LANES, :] = (qpad * (HEAD_DIM ** -0.5 * LOG2E)).astype(BF16)
        ca = _bf16_terms(2.0 ** -(h + 1) * LOG2E)
        cb = _bf16_terms(2.0 ** -(HG + h + 1) * LOG2E)
        c0, c1, c2 = [jnp.where(first, x, y) for x, y in zip(ca, cb)]
        qpad_ref[h, LANES:2 * LANES, :] = jnp.where(
            rid == 0, c0, jnp.where(rid == 1, c1, jnp.where(rid == 2, c2, 0.0))).astype(BF16)

    t_io = lax.broadcasted_iota(jnp.int32, (BLOCK_A, BLOCK_A), 0)
    u_io = lax.broadcasted_iota(jnp.int32, (BLOCK_A, BLOCK_A), 1)

    ones = jnp.ones((SUM_ROWS, BLOCK_A), BF16)

    def scores_head(j, slot, h, is_own=False):
        hp = h // 2
        k_aug = jnp.concatenate([kb_ref[j, :, LANES * hp:LANES * (hp + 1)], tpos_ref[...]], axis=1)
        s = jnp.dot(k_aug, qpad_ref[h], preferred_element_type=F32)
        if is_own:
            s = jnp.where(t_io <= u_io, s, NEG)
        s_refs[slot][h] = s

    def softmax_head(j, slot, h, valid=None):
        blk_max = jnp.max(s_refs[slot][h], axis=0, keepdims=True)
        if valid is None:
            alpha, m_new, m_eff = None, blk_max, blk_max
        else:
            off = ((j - own) * BLOCK_A).astype(F32)
            rb = (selb_ref[h, pl.ds(j, 1), :] + slope(h) * off) * LOG2E
            rb = jnp.where(valid, rb, NEG)
            m_old = m_ref[h]
            m_new = jnp.maximum(m_old, blk_max + rb)
            alpha = jnp.exp2(m_old - m_new)
            m_eff = m_new - rb
        m_ref[h] = m_new
        p_ref[h] = jnp.exp2(s_refs[slot][h] - m_eff).astype(BF16)
        return alpha

    def pv_head(j, h, alpha):
        v_aug = jnp.concatenate([vT_ref[j, HEAD_DIM * h:HEAD_DIM * (h + 1), :], ones], axis=0)
        pv = jnp.dot(v_aug, p_ref[h], preferred_element_type=F32)
        acc_ref[h] = pv if alpha is None else alpha * acc_ref[h] + pv

    def step(j_next, slot_next, j, slot, valid):
        alphas = [None] * HG
        for h in range(HG):
            scores_head(j_next, slot_next, h)
            alphas[h] = softmax_head(j, slot, h, valid)
            if h > 0:
                pv_head(j, h - 1, alphas[h - 1])
        pv_head(j, HG - 1, alphas[HG - 1])

    for h in range(HG):
        scores_head(own, 0, h, is_own=True)
    for h in range(HG):
        scores_head(0, 1, h)
        softmax_head(own, 0, h)
        pv_head(own, h, None)

    def steps(i, carry):
        for u in range(UNROLL):
            j = UNROLL * i + u
            step(jnp.minimum(j + 1, N_BLOCKS - 1), u % 2,
                 jnp.minimum(j, N_BLOCKS - 1), (u + 1) % 2, j < own)
        return carry

    lax.fori_loop(0, (own + UNROLL - 1) // UNROLL, steps, 0)

    outT = jnp.concatenate(
        [acc_ref[h, 0:HEAD_DIM, :] / acc_ref[h, HEAD_DIM:HEAD_DIM + 1, :] for h in range(HG)],
        axis=0)
    o_ref[...] = outT.T


def _attn_prompt(layer, pt_flat, qT, kb, vT, kmean, tpos, qkv_col, cache_kt2):
    gw = HG * HEAD_DIM
    row_of = lambda g, m: (g * N_BLOCKS + m) // STEPS_PER_ROW

    def page_spec(i):
        return pl.BlockSpec(
            (None, None, D_A, PAGE_SIZE),
            lambda g, m, pt: (layer, pt[(g * N_BLOCKS + m) * SPAGES + i], 0, 0))

    grid_spec = pltpu.PrefetchScalarGridSpec(
        num_scalar_prefetch=1,
        grid=(N_HEADS // HG, N_BLOCKS),
        in_specs=[
            pl.BlockSpec((gw, BLOCK_A), lambda g, m, pt: (g, m)),
            pl.BlockSpec((N_BLOCKS, BLOCK_A, gw), lambda g, m, pt: (0, 0, g),
                         pipeline_mode=pl.Buffered(1)),
            pl.BlockSpec((N_BLOCKS, gw, BLOCK_A), lambda g, m, pt: (0, g, 0),
                         pipeline_mode=pl.Buffered(1)),
            pl.BlockSpec((N_BLOCKS, gw), lambda g, m, pt: (0, g)),
            pl.BlockSpec((BLOCK_A, LANES), lambda g, m, pt: (0, 0)),
            pl.BlockSpec((1, D_A, 1), lambda g, m, pt: (row_of(g, m), 0, 0)),
        ] + [page_spec(i) for i in range(SPAGES)],
        out_specs=[
            pl.BlockSpec((BLOCK_A, gw), lambda g, m, pt: (m, g)),
            pl.BlockSpec((1, N_HEADS, LANES), lambda g, m, pt: (row_of(g, m), 0, 0)),
        ],
        scratch_shapes=[
            pltpu.VMEM((HG, 2 * LANES, BLOCK_A), BF16),
            pltpu.VMEM((HG, N_BLOCKS, BLOCK_A), F32),
            pltpu.VMEM((HG, 1, BLOCK_A), F32),
            pltpu.VMEM((HG, HEAD_DIM + SUM_ROWS, BLOCK_A), F32),
            pltpu.VMEM((HG, BLOCK_A, BLOCK_A), F32),
            pltpu.VMEM((HG, BLOCK_A, BLOCK_A), F32),
            pltpu.VMEM((HG, BLOCK_A, BLOCK_A), BF16),
        ],
    )
    return pl.pallas_call(
        _attn_kernel,
        grid_spec=grid_spec,
        out_shape=[
            jax.ShapeDtypeStruct((SEQ, D_A), F32),
            jax.ShapeDtypeStruct((DEC_BATCH, N_HEADS, LANES), F32),
        ],
        compiler_params=pltpu.CompilerParams(
            dimension_semantics=("arbitrary", "arbitrary"), vmem_limit_bytes=VMEM_LIMIT),
        name="attn_prompt",
    )(pt_flat, qT, kb, vT, kmean, tpos, qkv_col, *([cache_kt2] * SPAGES))


def _lru_gates(xc, wa, ba, wx, bx, lam):
    xcb = xc.astype(BF16)
    r = _sigmoid(jnp.dot(xcb, wa, preferred_element_type=F32) + ba)
    ig = _sigmoid(jnp.dot(xcb, wx, preferred_element_type=F32) + bx)
    nl = -lam
    softplus = jnp.maximum(nl, 0.0) + jnp.log1p(jnp.exp(-jnp.abs(nl)))
    log_a = (-C_LRU * softplus) * r
    a = jnp.exp(log_a)
    b = jnp.sqrt(-jnp.tanh(log_a) * (a * a + 1.0)) * (ig * xc)
    return a, b


def _layernorm(v, w, b):
    mu = jnp.mean(v, axis=-1, keepdims=True)
    d = v - mu
    var = jnp.mean(d * d, axis=-1, keepdims=True)
    return d * lax.rsqrt(var + EPS) * w + b


def _merge(x, a, b, c, ga, gb, gc, gm, wba, wbb, wbc, wo):
    pa = jnp.dot((a * _silu(ga)).astype(BF16), wba, preferred_element_type=F32)
    pb = jnp.dot((b * _silu(gb)).astype(BF16), wbb, preferred_element_type=F32)
    pc = jnp.dot((c * _silu(gc)).astype(BF16), wbc, preferred_element_type=F32)
    m = (_sigmoid(gm[:, 0:D_MODEL]) * pa + _sigmoid(gm[:, D_MODEL:2 * D_MODEL]) * pb
         + _sigmoid(gm[:, 2 * D_MODEL:3 * D_MODEL]) * pc)
    return x + jnp.dot(m.astype(BF16), wo, preferred_element_type=F32)


def _mix_kernel(final, x_ref, attn_ref, rest_ref, convw_ref, convb_ref, wa_ref, ba_ref, wx_ref,
                bx_ref, lam_ref, lnw_ref, lnb_ref, ws_ref, bsm_ref, wba_ref, wbb_ref, wbc_ref,
                wo_ref, fnw_ref, out_ref, xbtail_ref, htail_ref, ext_ref, hc_ref):
    i = pl.program_id(0)
    T = ROW_TILE

    @pl.when(i == 0)
    def _():
        ext_ref[0:8, :] = jnp.zeros((8, D_B), F32)
        hc_ref[...] = jnp.zeros((1, D_B), F32)

    xb = rest_ref[:, R_XB:R_XB + D_B]
    ext_ref[8:8 + T, :] = xb
    cw = convw_ref[...]
    xc = (convb_ref[...] + ext_ref[5:5 + T, :] * cw[0:1] + ext_ref[6:6 + T, :] * cw[1:2]
          + ext_ref[7:7 + T, :] * cw[2:3] + xb * cw[3:4])
    ext_ref[0:8, :] = xb[T - 8:T, :]
    xbtail_ref[...] = xb[T - 8:T, :]

    a, b = _lru_gates(xc, wa_ref[...], ba_ref[...], wx_ref[...], bx_ref[...], lam_ref[...])
    row = lax.broadcasted_iota(jnp.int32, (T, D_B), 0)
    d = 1
    while d < T:
        a_sh = jnp.where(row >= d, pltpu.roll(a, d, 0), 1.0)
        b_sh = jnp.where(row >= d, pltpu.roll(b, d, 0), 0.0)
        b = a * b_sh + b
        a = a * a_sh
        d *= 2
    hseq = b + a * hc_ref[...]
    hc_ref[...] = hseq[T - 1:T, :]
    htail_ref[...] = hseq[T - 8:T, :]

    vn = _layernorm(rest_ref[:, R_VC:R_VC + D_C], lnw_ref[...], lnb_ref[...]).astype(BF16)
    lane_g = lax.broadcasted_iota(jnp.int32, (CHUNK, D_C), 1) // (D_C // G_C)
    tri = (lax.broadcasted_iota(jnp.int32, (CHUNK, CHUNK), 0)
           >= lax.broadcasted_iota(jnp.int32, (CHUNK, CHUNK), 1))
    mixes = []
    for c in range(T // CHUNK):
        vch = vn[c * CHUNK:(c + 1) * CHUNK, :]
        mix = jnp.zeros((CHUNK, D_C), F32)
        for g in range(G_C):
            wg = jnp.where(tri, ws_ref[g], 0.0).astype(BF16)
            mix = jnp.where(lane_g == g, jnp.dot(wg, vch, preferred_element_type=F32), mix)
        mixes.append(mix + bsm_ref[...])
    cm = rest_ref[:, R_U:R_U + D_C] * jnp.concatenate(mixes, axis=0)

    xn = _merge(x_ref[...], attn_ref[...], hseq, cm,
                rest_ref[:, R_GA:R_GA + D_A], rest_ref[:, R_GB:R_GB + D_B],
                rest_ref[:, R_GC:R_GC + D_C], rest_ref[:, R_GM:R_GM + 3 * D_MODEL],
                wba_ref[...], wbb_ref[...], wbc_ref[...], wo_ref[...])
    out_ref[...] = _rms_rows(xn, fnw_ref[...]) if final else xn


def _mix_prompt(final, x, attn, rest, lw):
    n = SEQ // ROW_TILE
    row = lambda i: (i, 0)
    c2 = lambda i: (0, 0)
    c3 = lambda i: (0, 0, 0)
    vec = lambda d: pl.BlockSpec((1, d), c2)
    return pl.pallas_call(
        functools.partial(_mix_kernel, final),
        grid=(n,),
        in_specs=[
            pl.BlockSpec((ROW_TILE, D_MODEL), row),
            pl.BlockSpec((ROW_TILE, D_A), row),
            pl.BlockSpec((ROW_TILE, D_REST), row),
            pl.BlockSpec((4, D_B), c2), vec(D_B),
            pl.BlockSpec((D_B, D_B), c2), vec(D_B),
            pl.BlockSpec((D_B, D_B), c2), vec(D_B), vec(D_B),
            vec(D_C), vec(D_C),
            pl.BlockSpec((G_C, CHUNK, CHUNK), c3),
            pl.BlockSpec((CHUNK, D_C), c2),
            pl.BlockSpec((D_A, D_MODEL), c2),
            pl.BlockSpec((D_B, D_MODEL), c2),
            pl.BlockSpec((D_C, D_MODEL), c2),
            pl.BlockSpec((D_MODEL, D_MODEL), c2),
            vec(D_MODEL),
        ],
        out_specs=[
            pl.BlockSpec((ROW_TILE, D_MODEL), row),
            pl.BlockSpec((8, D_B), c2),
            pl.BlockSpec((8, D_B), c2),
        ],
        out_shape=[
            jax.ShapeDtypeStruct((SEQ, D_MODEL), F32),
            jax.ShapeDtypeStruct((8, D_B), F32),
            jax.ShapeDtypeStruct((8, D_B), F32),
        ],
        scratch_shapes=[
            pltpu.VMEM((8 + ROW_TILE, D_B), F32),
            pltpu.VMEM((1, D_B), F32),
        ],
        compiler_params=pltpu.CompilerParams(
            dimension_semantics=("arbitrary",), vmem_limit_bytes=VMEM_LIMIT),
        name="mix_prompt",
    )(x, attn, rest, lw["conv_w"], lw["conv_b"], lw["wa"], lw["ba"], lw["wx"], lw["bx"],
      lw["lam"], lw["ln_w"], lw["ln_b"], lw["ws"], lw["bs_mix"], lw["w_a"], lw["w_b"], lw["w_c"],
      lw["w_o"], lw["fnw"])


S_COLS = 1536


def _sproj_kernel(x_ref, nw_ref, w_ref, z_ref):
    h = _rms_rows(x_ref[...], nw_ref[...]).astype(BF16)
    z_ref[...] = jnp.dot(h, w_ref[...], preferred_element_type=F32)


def _proj_sample(x, nw, w_bf):
    return pl.pallas_call(
        _sproj_kernel,
        grid=(D_IN // S_COLS,),
        in_specs=[
            pl.BlockSpec((DEC_BATCH, D_MODEL), lambda j: (0, 0)),
            pl.BlockSpec((1, D_MODEL), lambda j: (0, 0)),
            pl.BlockSpec((D_MODEL, S_COLS), lambda j: (0, j)),
        ],
        out_specs=pl.BlockSpec((DEC_BATCH, S_COLS), lambda j: (0, j)),
        out_shape=jax.ShapeDtypeStruct((DEC_BATCH, D_IN), F32),
        compiler_params=pltpu.CompilerParams(dimension_semantics=("arbitrary",)),
        name="proj_sample",
    )(x, nw, w_bf)


def _sselect_kernel(g_ref, sel_ref):
    blk = lax.broadcasted_iota(jnp.int32, (N_HEADS, LANES), 1)
    g = jnp.where(blk < N_PAST_BLOCKS, g_ref[0], -jnp.inf)
    for r in range(N_SEL):
        mx = jnp.max(g, axis=1, keepdims=True)
        idx = jnp.min(jnp.where(g == mx, blk, LANES), axis=1, keepdims=True)
        sel_ref[0, r] = jnp.broadcast_to(idx, (N_HEADS, LANES))
        g = jnp.where(blk == idx, -jnp.inf, g)


def _select_sample(gate_s):
    return pl.pallas_call(
        _sselect_kernel,
        grid=(DEC_BATCH,),
        in_specs=[pl.BlockSpec((1, N_HEADS, LANES), lambda b: (b, 0, 0))],
        out_specs=pl.BlockSpec((1, N_SEL, N_HEADS, LANES), lambda b: (b, 0, 0, 0)),
        out_shape=jax.ShapeDtypeStruct((DEC_BATCH, N_SEL, N_HEADS, LANES), jnp.int32),
        compiler_params=pltpu.CompilerParams(dimension_semantics=("arbitrary",)),
        name="select_sample",
    )(gate_s)


N_SEL_PAGES = N_SEL * PAGES_PER_BLOCK


def _sattn_kernel(pt_ref, sel_ref, q_ref, kn_ref, vn_ref, *refs):
    kp = refs[:N_SEL_PAGES]
    vp = refs[N_SEL_PAGES:2 * N_SEL_PAGES]
    o_ref = refs[2 * N_SEL_PAGES]
    b = pl.program_id(0)
    h = pl.program_id(1)
    q = q_ref[0] * (HEAD_DIM ** -0.5)
    slope = jnp.exp2(-(jnp.full((1, 1), h, jnp.int32) + 1).astype(F32))
    t = lax.broadcasted_iota(jnp.int32, (1, PAGE_SIZE), 1).astype(F32)

    s_list = []
    for r in range(N_SEL):
        blk = sel_ref[(b * N_SEL + r) * N_HEADS + h]
        for hf in range(PAGES_PER_BLOCK):
            s = jnp.sum(kp[r * PAGES_PER_BLOCK + hf][...] * q, axis=0, keepdims=True)
            start = (PAST_LEN - blk * BLOCK_A - hf * PAGE_SIZE).astype(F32)
            s_list.append(s - slope * (start - t))
    s_own = jnp.sum(kn_ref[0] * q, axis=0, keepdims=True)

    m = s_own
    for s in s_list:
        m = jnp.maximum(m, jnp.max(s, axis=1, keepdims=True))
    p_own = jnp.exp(s_own - m)
    l = p_own
    acc = p_own * vn_ref[0]
    for i, s in enumerate(s_list):
        p = jnp.exp(s - m)
        l = l + jnp.sum(p, axis=1, keepdims=True)
        acc = acc + jnp.sum(vp[i][...] * p, axis=1, keepdims=True)
    o_ref[0, 0] = acc / l


def _attn_sample(layer, pt_flat, sel_flat, qkv_col, cache_kt, cache_vt):
    def page_spec(r, hf):
        return pl.BlockSpec(
            (None, None, None, HEAD_DIM, PAGE_SIZE),
            lambda b, h, pt, sel: (
                layer,
                pt[b * N_PAGES + sel[(b * N_SEL + r) * N_HEADS + h] * PAGES_PER_BLOCK + hf],
                h, 0, 0))

    def col(part):
        return pl.BlockSpec((1, HEAD_DIM, 1), lambda b, h, pt, sel: (b, part * N_HEADS + h, 0))

    pages = [page_spec(r, hf) for r in range(N_SEL) for hf in range(PAGES_PER_BLOCK)]
    grid_spec = pltpu.PrefetchScalarGridSpec(
        num_scalar_prefetch=2,
        grid=(DEC_BATCH, N_HEADS),
        in_specs=[col(0), col(1), col(2)] + pages + pages,
        out_specs=pl.BlockSpec((1, 1, HEAD_DIM, 1), lambda b, h, pt, sel: (b, h, 0, 0)),
    )
    return pl.pallas_call(
        _sattn_kernel,
        grid_spec=grid_spec,
        out_shape=jax.ShapeDtypeStruct((DEC_BATCH, N_HEADS, HEAD_DIM, 1), F32),
        compiler_params=pltpu.CompilerParams(dimension_semantics=("arbitrary", "arbitrary")),
        name="attn_sample",
    )(pt_flat, sel_flat, qkv_col, qkv_col, qkv_col,
      *([cache_kt] * N_SEL_PAGES), *([cache_vt] * N_SEL_PAGES))


def _smix_kernel(final, x_ref, attn_ref, z_ref, sc0_ref, sc1_ref, sc2_ref, h0_ref, convw_ref,
                 convb_ref, wa_ref, ba_ref, wx_ref, bx_ref, lam_ref, lnw_ref, lnb_ref, ws0_ref,
                 bs0_ref, wba_ref, wbb_ref, wbc_ref, wo_ref, fnw_ref, out_ref, hnew_ref, vn_ref):
    o = 3 * D_A
    xb = z_ref[:, o + R_XB:o + R_XB + D_B]
    cw = convw_ref[...]
    xc = (convb_ref[...] + sc0_ref[...] * cw[0:1] + sc1_ref[...] * cw[1:2]
          + sc2_ref[...] * cw[2:3] + xb * cw[3:4])
    a, b = _lru_gates(xc, wa_ref[...], ba_ref[...], wx_ref[...], bx_ref[...], lam_ref[...])
    hnew = b + a * h0_ref[...]
    hnew_ref[...] = hnew

    vn = _layernorm(_gelu(z_ref[:, o + R_VC:o + R_VC + D_C]), lnw_ref[...], lnb_ref[...])
    vn_ref[...] = vn
    cm = _gelu(z_ref[:, o + R_U:o + R_U + D_C]) * (ws0_ref[...] * vn + bs0_ref[...])

    xn = _merge(x_ref[...], attn_ref[...], hnew, cm,
                z_ref[:, o + R_GA:o + R_GA + D_A], z_ref[:, o + R_GB:o + R_GB + D_B],
                z_ref[:, o + R_GC:o + R_GC + D_C], z_ref[:, o + R_GM:o + R_GM + 3 * D_MODEL],
                wba_ref[...], wbb_ref[...], wbc_ref[...], wo_ref[...])
    out_ref[...] = _rms_rows(xn, fnw_ref[...]) if final else xn


def _mix_sample(final, x, attn, z, sc0, sc1, sc2, h0, lw):
    return pl.pallas_call(
        functools.partial(_smix_kernel, final),
        out_shape=[
            jax.ShapeDtypeStruct((DEC_BATCH, D_MODEL), F32),
            jax.ShapeDtypeStruct((DEC_BATCH, D_B), F32),
            jax.ShapeDtypeStruct((DEC_BATCH, D_C), F32),
        ],
        compiler_params=pltpu.CompilerParams(vmem_limit_bytes=VMEM_LIMIT),
        name="mix_sample",
    )(x, attn, z, sc0, sc1, sc2, h0, lw["conv_w"], lw["conv_b"], lw["wa"], lw["ba"], lw["wx"],
      lw["bx"], lw["lam"], lw["ln_w"], lw["ln_b"], lw["ws0"], lw["bs0"], lw["w_a"], lw["w_b"],
      lw["w_c"], lw["w_o"], lw["fnw"])


def _block_diag(w):
    g, n, _ = w.shape
    eye = jnp.eye(g, dtype=w.dtype)
    return (w[:, :, None, :] * eye[:, None, :, None]).reshape(g * n, g * n)


def _heads_last(xT):
    return xT.reshape(DEPTH, N_HEADS, HEAD_DIM, -1).transpose(0, 3, 1, 2)[:, None]


def kernel(x_prompt, x_sample, cache_k, cache_v, page_table, state_lru_h, state_conv, norm_w, w_in, w_br_a, w_br_b, w_br_c, w_out, conv_w, conv_b, lru_wa, lru_ba, lru_wx, lru_bx, lru_lambda, gmlp_ln_w, gmlp_ln_b, gmlp_ws, gmlp_bs, final_norm_w):
    n_pool = cache_k.shape[1]
    cache_kt = cache_k.transpose(0, 1, 3, 4, 2)
    cache_vt = cache_v.transpose(0, 1, 3, 4, 2)
    cache_kt2 = cache_kt.reshape(DEPTH, n_pool, D_A, PAGE_SIZE)
    pt_flat = page_table.reshape(-1)
    tpos = jnp.where(jnp.arange(LANES)[None, :] < 3, jnp.arange(BLOCK_A, dtype=F32)[:, None], 0.0).astype(BF16)
    fnw = final_norm_w.reshape(1, D_MODEL)

    xp = x_prompt.reshape(SEQ, D_MODEL)
    xs = x_sample.reshape(DEC_BATCH, D_MODEL)
    outs = {k: [] for k in ("ks", "vs", "hp", "cp", "hs", "cs", "cv")}
    kv_stacks = ()
    for l in range(DEPTH):
        final = l == DEPTH - 1
        w_bf = w_in[l].astype(BF16)
        nw = norm_w[l].reshape(1, D_MODEL)
        lw = dict(
            conv_w=conv_w[l], conv_b=conv_b[l].reshape(1, D_B),
            wa=_block_diag(lru_wa[l]).astype(BF16), ba=lru_ba[l].reshape(1, D_B),
            wx=_block_diag(lru_wx[l]).astype(BF16), bx=lru_bx[l].reshape(1, D_B),
            lam=lru_lambda[l].reshape(1, D_B),
            ln_w=gmlp_ln_w[l].reshape(1, D_C), ln_b=gmlp_ln_b[l].reshape(1, D_C),
            ws=gmlp_ws[l], bs_mix=jnp.repeat(gmlp_bs[l].T, D_C // G_C, axis=1),
            ws0=jnp.repeat(gmlp_ws[l][:, 0, 0], D_C // G_C).reshape(1, D_C),
            bs0=jnp.repeat(gmlp_bs[l][:, 0], D_C // G_C).reshape(1, D_C),
            w_a=w_br_a[l].astype(BF16), w_b=w_br_b[l].astype(BF16), w_c=w_br_c[l].astype(BF16),
            w_o=w_out[l].astype(BF16), fnw=fnw)

        z = _proj_sample(xs, nw, w_bf)
        kn = z[:, D_A:2 * D_A]
        vn = z[:, 2 * D_A:3 * D_A]
        qkv_col = z[:, 0:3 * D_A].reshape(DEC_BATCH, 3 * D_A, 1)

        qT, kT, vT, kb, vTb, kmean, rest = _proj_prompt(l, xp, nw, w_bf, kv_stacks)
        kv_stacks = (kT, vT)
        attn, gate_s = _attn_prompt(l, pt_flat, qT, kb, vTb, kmean.reshape(N_BLOCKS, D_A), tpos,
                                    qkv_col, cache_kt2)
        xp, xbtail, htail = _mix_prompt(final, xp, attn, rest, lw)
        outs["hp"].append(htail[7:8])
        outs["cp"].append(xbtail[5:8].reshape(1, 3, D_B))

        sel = _select_sample(gate_s)
        sel_flat = sel[:, :, :, 0].reshape(-1)
        attn_s = _attn_sample(l, pt_flat, sel_flat, qkv_col, cache_kt, cache_vt)
        sc = state_conv[l]
        xs, hnew, cvn = _mix_sample(final, xs, attn_s.reshape(DEC_BATCH, D_A), z,
                                    sc[:, 0], sc[:, 1], sc[:, 2], state_lru_h[l], lw)
        xb_s = z[:, 3 * D_A + R_XB:3 * D_A + R_XB + D_B]
        outs["ks"].append(kn.reshape(DEC_BATCH, 1, N_HEADS, HEAD_DIM))
        outs["vs"].append(vn.reshape(DEC_BATCH, 1, N_HEADS, HEAD_DIM))
        outs["hs"].append(hnew)
        outs["cs"].append(jnp.stack([sc[:, 1], sc[:, 2], xb_s], axis=1))
        outs["cv"].append(cvn.reshape(DEC_BATCH, 1, D_C))

    st = lambda k: jnp.stack(outs[k])
    return (xp.reshape(1, SEQ, D_MODEL), xs.reshape(DEC_BATCH, 1, D_MODEL),
            _heads_last(kv_stacks[0]), _heads_last(kv_stacks[1]), st("ks"), st("vs"), st("hp"), st("cp"), st("hs"), st("cs"), st("cv"))
```

```python
import functools

import jax
import jax.numpy as jnp
import numpy as np
from jax import lax
from jax.experimental import pallas as pl
from jax.experimental.pallas import tpu as pltpu

F32 = jnp.float32
BF16 = jnp.bfloat16
HIGHEST = lax.Precision.HIGHEST

D_MODEL = 1024
SEQ = 16384
DEPTH = 2
DEC_BATCH = 32
PAST_LEN = 16384
PAGE_SIZE = 128
N_HEADS = 8
HEAD_DIM = 64
D_A = N_HEADS * HEAD_DIM
BLOCK_A = 256
N_SEL = 3
D_B = 512
G_B = 8
C_LRU = 8.0
D_C = 512
G_C = 8
CHUNK = 128
D_IN = 4 * D_A + 2 * D_B + 3 * D_C + 3 * D_MODEL
EPS = 1e-6

N_BLOCKS = SEQ // BLOCK_A
N_PAGES = PAST_LEN // PAGE_SIZE
PAGES_PER_BLOCK = BLOCK_A // PAGE_SIZE
N_PAST_BLOCKS = PAST_LEN // BLOCK_A
NEG = -1e30
LOG2E = 1.4426950408889634
ROW_TILE = 256
LANES = 128
HG = 4
SPAGES = 32
STEPS_PER_ROW = N_PAGES // SPAGES
assert (N_HEADS // HG) * N_BLOCKS * SPAGES == DEC_BATCH * N_PAGES
UNROLL = 4
SCAN_GROUP = 8
SUM_ROWS = 16
VMEM_LIMIT = 52 * 1024 * 1024

R_GA, R_XB, R_GB, R_U, R_VC, R_GC, R_GM = 0, 512, 1024, 1536, 2048, 2560, 3072
D_REST = D_IN - 3 * D_A


def _gelu(x):
    return 0.5 * x * (1.0 + jnp.tanh(0.7978845608028654 * (x + 0.044715 * (x * x * x))))


def _sigmoid(x):
    return jax.nn.sigmoid(x)


def _silu(x):
    return x * _sigmoid(x)


def _rms_rows(x, w):
    return x * lax.rsqrt(jnp.mean(x * x, axis=-1, keepdims=True) + EPS) * w


def _proj_kernel(n_prev, x_ref, nw_ref, w_ref, *refs):
    qT_ref, kT_ref, vT_ref, kb_ref, vTb_ref, kmean_ref, rest_ref = refs[n_prev:]
    h = _rms_rows(x_ref[...], nw_ref[...]).astype(BF16)

    def proj(a, b):
        return jnp.dot(h, w_ref[:, a:b], preferred_element_type=F32)

    qT_ref[...] = proj(0, D_A).T
    k = proj(D_A, 2 * D_A)
    kT_ref[...] = k.T
    kb_ref[0] = k.astype(BF16)
    kmean_ref[0] = jnp.mean(k, axis=0, keepdims=True)
    vT = proj(2 * D_A, 3 * D_A).T
    vT_ref[...] = vT
    vTb_ref[0] = vT.astype(BF16)
    o = 3 * D_A
    rest_ref[:, R_GA:R_U] = proj(o + R_GA, o + R_U)
    rest_ref[:, R_U:R_GC] = _gelu(proj(o + R_U, o + R_GC))
    rest_ref[:, R_GC:D_REST] = proj(o + R_GC, o + D_REST)


def _proj_prompt(layer, x, nw, w_bf, kv_stacks):
    n = SEQ // ROW_TILE
    const = lambda i: (0, 0)
    colblk = pl.BlockSpec((D_A, ROW_TILE), lambda i: (0, i))
    stackblk = pl.BlockSpec((None, D_A, ROW_TILE), lambda i: (layer, 0, i))
    stack_shape = jax.ShapeDtypeStruct((DEPTH, D_A, SEQ), F32)
    n_prev = len(kv_stacks)
    return pl.pallas_call(
        functools.partial(_proj_kernel, n_prev),
        grid=(n,),
        in_specs=[
            pl.BlockSpec((ROW_TILE, D_MODEL), lambda i: (i, 0)),
            pl.BlockSpec((1, D_MODEL), const),
            pl.BlockSpec((D_MODEL, D_IN), const, pipeline_mode=pl.Buffered(1)),
        ] + [pl.BlockSpec(memory_space=pl.ANY)] * n_prev,
        input_output_aliases={3 + i: 1 + i for i in range(n_prev)},
        out_specs=[
            colblk, stackblk, stackblk,
            pl.BlockSpec((1, ROW_TILE, D_A), lambda i: (i, 0, 0)),
            pl.BlockSpec((1, D_A, ROW_TILE), lambda i: (i, 0, 0)),
            pl.BlockSpec((1, 1, D_A), lambda i: (i, 0, 0)),
            pl.BlockSpec((ROW_TILE, D_REST), lambda i: (i, 0)),
        ],
        out_shape=[
            jax.ShapeDtypeStruct((D_A, SEQ), F32),
            stack_shape,
            stack_shape,
            jax.ShapeDtypeStruct((n, ROW_TILE, D_A), BF16),
            jax.ShapeDtypeStruct((n, D_A, ROW_TILE), BF16),
            jax.ShapeDtypeStruct((n, 1, D_A), F32),
            jax.ShapeDtypeStruct((SEQ, D_REST), F32),
        ],
        compiler_params=pltpu.CompilerParams(
            dimension_semantics=("arbitrary",), vmem_limit_bytes=VMEM_LIMIT),
        name="proj_prompt",
    )(x, nw, w_bf, *kv_stacks)


def _bf16_terms(c, n=3):
    terms, rest = [], float(c)
    for _ in range(n):
        t = float(np.asarray(rest, dtype=np.float32).astype(BF16).astype(np.float32))
        terms.append(t)
        rest -= t
    return terms


def _attn_kernel(pt_ref, qT_ref, kb_ref, vT_ref, kmean_ref, tpos_ref, qs_ref, *refs):
    pages = refs[:SPAGES]
    o_ref, gs_ref = refs[SPAGES:SPAGES + 2]
    qpad_ref, selb_ref, m_ref, acc_ref, sa_ref, sb_ref, p_ref = refs[SPAGES + 2:]
    s_refs = (sa_ref, sb_ref)
    hg = pl.program_id(0)
    own = pl.program_id(1)

    part = (hg * N_BLOCKS + own) % STEPS_PER_ROW
    nb = SPAGES // PAGES_PER_BLOCK
    lane = lax.broadcasted_iota(jnp.int32, (N_HEADS, LANES), 1)

    @pl.when(part == 0)
    def _():
        gs_ref[0] = jnp.zeros((N_HEADS, LANES), F32)

    slab = 2 * HEAD_DIM
    per_head = [[] for _ in range(nb)]
    for sl in range(D_A // slab):
        q_slab = jnp.broadcast_to(qs_ref[0, slab * sl:slab * (sl + 1), :], (slab, PAGE_SIZE))
        for i in range(nb):
            both = (pages[PAGES_PER_BLOCK * i][slab * sl:slab * (sl + 1), :]
                    + pages[PAGES_PER_BLOCK * i + 1][slab * sl:slab * (sl + 1), :])
            per_head[i].append(jnp.sum((both * q_slab).reshape(2, HEAD_DIM, PAGE_SIZE), axis=1))
    gt = gs_ref[0]
    for i in range(nb):
        g_col = jnp.sum(jnp.concatenate(per_head[i], axis=0), axis=1, keepdims=True) * (1.0 / BLOCK_A)
        gt = jnp.where(lane == part * nb + i, g_col, gt)
    gs_ref[0] = gt

    first = hg == 0
    qT = qT_ref[...]
    blk = lax.broadcasted_iota(jnp.int32, (N_BLOCKS, BLOCK_A), 0)
    rid = lax.broadcasted_iota(jnp.int32, (2 * HEAD_DIM, BLOCK_A), 0)
    half = rid // HEAD_DIM

    def slope(h):
        return jnp.where(first, 2.0 ** -(h + 1), 2.0 ** -(HG + h + 1))

    for h in range(HG):
        hp, par = divmod(h, 2)
        rows = qT[LANES * hp:LANES * (hp + 1), :]
        qpad = jnp.where(half == par, rows, 0.0)
        g = jnp.dot(kmean_ref[:, LANES * hp:LANES * (hp + 1)], qpad,
                    precision=HIGHEST, preferred_element_type=F32)
        g = jnp.where(blk < own, g, -jnp.inf)
        selb = jnp.full((N_BLOCKS, BLOCK_A), NEG, F32)
        for r in range(N_SEL):
            mx = jnp.max(g, axis=0, keepdims=True)
            idx = jnp.min(jnp.where(g == mx, blk, N_BLOCKS), axis=0, keepdims=True)
            hit = blk == idx
            selb = jnp.where(jnp.logical_and(hit, r < own), 0.0, selb)
            g = jnp.where(hit, -jnp.inf, g)
        selb_ref[h] = selb
        qpad_ref[h, 0:LANES, :] = (qpad * (HEAD_DIM ** -0.5 * LOG2E)).astype(BF16)
        ca = _bf16_terms(2.0 ** -(h + 1) * LOG2E)
        cb = _bf16_terms(2.0 ** -(HG + h + 1) * LOG2E)
        c0, c1, c2 = [jnp.where(first, x, y) for x, y in zip(ca, cb)]
        qpad_ref[h, LANES:2 * LANES, :] = jnp.where(
            rid == 0, c0, jnp.where(rid == 1, c1, jnp.where(rid == 2, c2, 0.0))).astype(BF16)

    t_io = lax.broadcasted_iota(jnp.int32, (BLOCK_A, BLOCK_A), 0)
    u_io = lax.broadcasted_iota(jnp.int32, (BLOCK_A, BLOCK_A), 1)

    ones = jnp.ones((SUM_ROWS, BLOCK_A), BF16)

    def scores_head(j, slot, h, is_own=False):
        hp = h // 2
        k_aug = jnp.concatenate([kb_ref[j, :, LANES * hp:LANES * (hp + 1)], tpos_ref[...]], axis=1)
        s = jnp.dot(k_aug, qpad_ref[h], preferred_element_type=F32)
        if is_own:
            s = jnp.where(t_io <= u_io, s, NEG)
        s_refs[slot][h] = s

    def softmax_head(j, slot, h, valid=None):
        blk_max = jnp.max(s_refs[slot][h], axis=0, keepdims=True)
        if valid is None:
            alpha, m_new, m_eff = None, blk_max, blk_max
        else:
            off = ((j - own) * BLOCK_A).astype(F32)
            rb = (selb_ref[h, pl.ds(j, 1), :] + slope(h) * off) * LOG2E
            rb = jnp.where(valid, rb, NEG)
            m_old = m_ref[h]
            m_new = jnp.maximum(m_old, blk_max + rb)
            alpha = jnp.exp2(m_old - m_new)
            m_eff = m_new - rb
        m_ref[h] = m_new
        p_ref[h] = jnp.exp2(s_refs[slot][h] - m_eff).astype(BF16)
        return alpha

    def pv_head(j, h, alpha):
        v_aug = jnp.concatenate([vT_ref[j, HEAD_DIM * h:HEAD_DIM * (h + 1), :], ones], axis=0)
        pv = jnp.dot(v_aug, p_ref[h], preferred_element_type=F32)
        acc_ref[h] = pv if alpha is None else alpha * acc_ref[h] + pv

    def step(j_next, slot_next, j, slot, valid):
        alphas = [None] * HG
        for h in range(HG):
            scores_head(j_next, slot_next, h)
            alphas[h] = softmax_head(j, slot, h, valid)
            if h > 0:
                pv_head(j, h - 1, alphas[h - 1])
        pv_head(j, HG - 1, alphas[HG - 1])

    for h in range(HG):
        scores_head(own, 0, h, is_own=True)
    for h in range(HG):
        scores_head(0, 1, h)
        softmax_head(own, 0, h)
        pv_head(own, h, None)

    def steps(i, carry):
        for u in range(UNROLL):
            j = UNROLL * i + u
            step(jnp.minimum(j + 1, N_BLOCKS - 1), u % 2,
                 jnp.minimum(j, N_BLOCKS - 1), (u + 1) % 2, j < own)
        return carry

    lax.fori_loop(0, (own + UNROLL - 1) // UNROLL, steps, 0)

    outT = jnp.concatenate(
        [acc_ref[h, 0:HEAD_DIM, :] / acc_ref[h, HEAD_DIM:HEAD_DIM + 1, :] for h in range(HG)],
        axis=0)
    o_ref[...] = outT.T


def _attn_prompt(layer, pt_flat, qT, kb, vT, kmean, tpos, qkv_col, cache_kt2):
    gw = HG * HEAD_DIM
    row_of = lambda g, m: (g * N_BLOCKS + m) // STEPS_PER_ROW

    def page_spec(i):
        return pl.BlockSpec(
            (None, None, D_A, PAGE_SIZE),
            lambda g, m, pt: (layer, pt[(g * N_BLOCKS + m) * SPAGES + i], 0, 0))

    grid_spec = pltpu.PrefetchScalarGridSpec(
        num_scalar_prefetch=1,
        grid=(N_HEADS // HG, N_BLOCKS),
        in_specs=[
            pl.BlockSpec((gw, BLOCK_A), lambda g, m, pt: (g, m)),
            pl.BlockSpec((N_BLOCKS, BLOCK_A, gw), lambda g, m, pt: (0, 0, g),
                         pipeline_mode=pl.Buffered(1)),
            pl.BlockSpec((N_BLOCKS, gw, BLOCK_A), lambda g, m, pt: (0, g, 0),
                         pipeline_mode=pl.Buffered(1)),
            pl.BlockSpec((N_BLOCKS, gw), lambda g, m, pt: (0, g)),
            pl.BlockSpec((BLOCK_A, LANES), lambda g, m, pt: (0, 0)),
            pl.BlockSpec((1, D_A, 1), lambda g, m, pt: (row_of(g, m), 0, 0)),
        ] + [page_spec(i) for i in range(SPAGES)],
        out_specs=[
            pl.BlockSpec((BLOCK_A, gw), lambda g, m, pt: (m, g)),
            pl.BlockSpec((1, N_HEADS, LANES), lambda g, m, pt: (row_of(g, m), 0, 0)),
        ],
        scratch_shapes=[
            pltpu.VMEM((HG, 2 * LANES, BLOCK_A), BF16),
            pltpu.VMEM((HG, N_BLOCKS, BLOCK_A), F32),
            pltpu.VMEM((HG, 1, BLOCK_A), F32),
            pltpu.VMEM((HG, HEAD_DIM + SUM_ROWS, BLOCK_A), F32),
            pltpu.VMEM((HG, BLOCK_A, BLOCK_A), F32),
            pltpu.VMEM((HG, BLOCK_A, BLOCK_A), F32),
            pltpu.VMEM((HG, BLOCK_A, BLOCK_A), BF16),
        ],
    )
    return pl.pallas_call(
        _attn_kernel,
        grid_spec=grid_spec,
        out_shape=[
            jax.ShapeDtypeStruct((SEQ, D_A), F32),
            jax.ShapeDtypeStruct((DEC_BATCH, N_HEADS, LANES), F32),
        ],
        compiler_params=pltpu.CompilerParams(
            dimension_semantics=("arbitrary", "arbitrary"), vmem_limit_bytes=VMEM_LIMIT),
        name="attn_prompt",
    )(pt_flat, qT, kb, vT, kmean, tpos, qkv_col, *([cache_kt2] * SPAGES))


def _lru_gates(xc, wa, ba, wx, bx, lam):
    xcb = xc.astype(BF16)
    r = _sigmoid(jnp.dot(xcb, wa, preferred_element_type=F32) + ba)
    ig = _sigmoid(jnp.dot(xcb, wx, preferred_element_type=F32) + bx)
    nl = -lam
    softplus = jnp.maximum(nl, 0.0) + jnp.log1p(jnp.exp(-jnp.abs(nl)))
    log_a = (-C_LRU * softplus) * r
    a = jnp.exp(log_a)
    b = jnp.sqrt(-jnp.tanh(log_a) * (a * a + 1.0)) * (ig * xc)
    return a, b


def _layernorm(v, w, b):
    mu = jnp.mean(v, axis=-1, keepdims=True)
    d = v - mu
    var = jnp.mean(d * d, axis=-1, keepdims=True)
    return d * lax.rsqrt(var + EPS) * w + b


def _merge(x, a, b, c, ga, gb, gc, gm, wba, wbb, wbc, wo):
    pa = jnp.dot((a * _silu(ga)).astype(BF16), wba, preferred_element_type=F32)
    pb = jnp.dot((b * _silu(gb)).astype(BF16), wbb, preferred_element_type=F32)
    pc = jnp.dot((c * _silu(gc)).astype(BF16), wbc, preferred_element_type=F32)
    m = (_sigmoid(gm[:, 0:D_MODEL]) * pa + _sigmoid(gm[:, D_MODEL:2 * D_MODEL]) * pb
         + _sigmoid(gm[:, 2 * D_MODEL:3 * D_MODEL]) * pc)
    return x + jnp.dot(m.astype(BF16), wo, preferred_element_type=F32)


def _mix_kernel(final, x_ref, attn_ref, rest_ref, convw_ref, convb_ref, wa_ref, ba_ref, wx_ref,
                bx_ref, lam_ref, lnw_ref, lnb_ref, ws_ref, bsm_ref, wba_ref, wbb_ref, wbc_ref,
                wo_ref, fnw_ref, out_ref, xbtail_ref, htail_ref, ext_ref, hc_ref):
    i = pl.program_id(0)
    T = ROW_TILE

    @pl.when(i == 0)
    def _():
        ext_ref[0:8, :] = jnp.zeros((8, D_B), F32)
        hc_ref[...] = jnp.zeros((1, D_B), F32)

    xb = rest_ref[:, R_XB:R_XB + D_B]
    ext_ref[8:8 + T, :] = xb
    cw = convw_ref[...]
    xc = (convb_ref[...] + ext_ref[5:5 + T, :] * cw[0:1] + ext_ref[6:6 + T, :] * cw[1:2]
          + ext_ref[7:7 + T, :] * cw[2:3] + xb * cw[3:4])
    ext_ref[0:8, :] = xb[T - 8:T, :]
    xbtail_ref[...] = xb[T - 8:T, :]

    a, b = _lru_gates(xc, wa_ref[...], ba_ref[...], wx_ref[...], bx_ref[...], lam_ref[...])
    grp = lax.broadcasted_iota(jnp.int32, (T, D_B), 0) % SCAN_GROUP
    d = 1
    while d < SCAN_GROUP:
        a_sh = jnp.where(grp >= d, pltpu.roll(a, d, 0), 1.0)
        b_sh = jnp.where(grp >= d, pltpu.roll(b, d, 0), 0.0)
        b = a * b_sh + b
        a = a * a_sh
        d *= 2
    carry = hc_ref[...]
    groups = []
    for g in range(T // SCAN_GROUP):
        rows = slice(SCAN_GROUP * g, SCAN_GROUP * (g + 1))
        hg = b[rows] + a[rows] * carry
        groups.append(hg)
        carry = hg[SCAN_GROUP - 1:SCAN_GROUP, :]
    hseq = jnp.concatenate(groups, axis=0)
    hc_ref[...] = carry
    htail_ref[...] = hseq[T - 8:T, :]

    vn = _layernorm(rest_ref[:, R_VC:R_VC + D_C], lnw_ref[...], lnb_ref[...]).astype(BF16)
    lane_g = lax.broadcasted_iota(jnp.int32, (CHUNK, D_C), 1) // (D_C // G_C)
    tri = (lax.broadcasted_iota(jnp.int32, (CHUNK, CHUNK), 0)
           >= lax.broadcasted_iota(jnp.int32, (CHUNK, CHUNK), 1))
    mixes = []
    for c in range(T // CHUNK):
        vch = vn[c * CHUNK:(c + 1) * CHUNK, :]
        mix = jnp.zeros((CHUNK, D_C), F32)
        for g in range(G_C):
            wg = jnp.where(tri, ws_ref[g], 0.0).astype(BF16)
            mix = jnp.where(lane_g == g, jnp.dot(wg, vch, preferred_element_type=F32), mix)
        mixes.append(mix + bsm_ref[...])
    cm = rest_ref[:, R_U:R_U + D_C] * jnp.concatenate(mixes, axis=0)

    xn = _merge(x_ref[...], attn_ref[...], hseq, cm,
                rest_ref[:, R_GA:R_GA + D_A], rest_ref[:, R_GB:R_GB + D_B],
                rest_ref[:, R_GC:R_GC + D_C], rest_ref[:, R_GM:R_GM + 3 * D_MODEL],
                wba_ref[...], wbb_ref[...], wbc_ref[...], wo_ref[...])
    out_ref[...] = _rms_rows(xn, fnw_ref[...]) if final else xn


def _mix_prompt(final, x, attn, rest, lw):
    n = SEQ // ROW_TILE
    row = lambda i: (i, 0)
    c2 = lambda i: (0, 0)
    c3 = lambda i: (0, 0, 0)
    vec = lambda d: pl.BlockSpec((1, d), c2)
    return pl.pallas_call(
        functools.partial(_mix_kernel, final),
        grid=(n,),
        in_specs=[
            pl.BlockSpec((ROW_TILE, D_MODEL), row),
            pl.BlockSpec((ROW_TILE, D_A), row),
            pl.BlockSpec((ROW_TILE, D_REST), row),
            pl.BlockSpec((4, D_B), c2), vec(D_B),
            pl.BlockSpec((D_B, D_B), c2), vec(D_B),
            pl.BlockSpec((D_B, D_B), c2), vec(D_B), vec(D_B),
            vec(D_C), vec(D_C),
            pl.BlockSpec((G_C, CHUNK, CHUNK), c3),
            pl.BlockSpec((CHUNK, D_C), c2),
            pl.BlockSpec((D_A, D_MODEL), c2),
            pl.BlockSpec((D_B, D_MODEL), c2),
            pl.BlockSpec((D_C, D_MODEL), c2),
            pl.BlockSpec((D_MODEL, D_MODEL), c2),
            vec(D_MODEL),
        ],
        out_specs=[
            pl.BlockSpec((ROW_TILE, D_MODEL), row),
            pl.BlockSpec((8, D_B), c2),
            pl.BlockSpec((8, D_B), c2),
        ],
        out_shape=[
            jax.ShapeDtypeStruct((SEQ, D_MODEL), F32),
            jax.ShapeDtypeStruct((8, D_B), F32),
            jax.ShapeDtypeStruct((8, D_B), F32),
        ],
        scratch_shapes=[
            pltpu.VMEM((8 + ROW_TILE, D_B), F32),
            pltpu.VMEM((1, D_B), F32),
        ],
        compiler_params=pltpu.CompilerParams(
            dimension_semantics=("arbitrary",), vmem_limit_bytes=VMEM_LIMIT),
        name="mix_prompt",
    )(x, attn, rest, lw["conv_w"], lw["conv_b"], lw["wa"], lw["ba"], lw["wx"], lw["bx"],
      lw["lam"], lw["ln_w"], lw["ln_b"], lw["ws"], lw["bs_mix"], lw["w_a"], lw["w_b"], lw["w_c"],
      lw["w_o"], lw["fnw"])


S_COLS = 1536


def _sproj_kernel(x_ref, nw_ref, w_ref, z_ref):
    h = _rms_rows(x_ref[...], nw_ref[...]).astype(BF16)
    z_ref[...] = jnp.dot(h, w_ref[...], preferred_element_type=F32)


def _proj_sample(x, nw, w_bf):
    return pl.pallas_call(
        _sproj_kernel,
        grid=(D_IN // S_COLS,),
        in_specs=[
            pl.BlockSpec((DEC_BATCH, D_MODEL), lambda j: (0, 0)),
            pl.BlockSpec((1, D_MODEL), lambda j: (0, 0)),
            pl.BlockSpec((D_MODEL, S_COLS), lambda j: (0, j)),
        ],
        out_specs=pl.BlockSpec((DEC_BATCH, S_COLS), lambda j: (0, j)),
        out_shape=jax.ShapeDtypeStruct((DEC_BATCH, D_IN), F32),
        compiler_params=pltpu.CompilerParams(dimension_semantics=("arbitrary",)),
        name="proj_sample",
    )(x, nw, w_bf)


def _sselect_kernel(g_ref, sel_ref):
    blk = lax.broadcasted_iota(jnp.int32, (N_HEADS, LANES), 1)
    g = jnp.where(blk < N_PAST_BLOCKS, g_ref[0], -jnp.inf)
    for r in range(N_SEL):
        mx = jnp.max(g, axis=1, keepdims=True)
        idx = jnp.min(jnp.where(g == mx, blk, LANES), axis=1, keepdims=True)
        sel_ref[0, r] = jnp.broadcast_to(idx, (N_HEADS, LANES))
        g = jnp.where(blk == idx, -jnp.inf, g)


def _select_sample(gate_s):
    return pl.pallas_call(
        _sselect_kernel,
        grid=(DEC_BATCH,),
        in_specs=[pl.BlockSpec((1, N_HEADS, LANES), lambda b: (b, 0, 0))],
        out_specs=pl.BlockSpec((1, N_SEL, N_HEADS, LANES), lambda b: (b, 0, 0, 0)),
        out_shape=jax.ShapeDtypeStruct((DEC_BATCH, N_SEL, N_HEADS, LANES), jnp.int32),
        compiler_params=pltpu.CompilerParams(dimension_semantics=("arbitrary",)),
        name="select_sample",
    )(gate_s)


N_SEL_PAGES = N_SEL * PAGES_PER_BLOCK


def _sattn_kernel(pt_ref, sel_ref, q_ref, kn_ref, vn_ref, *refs):
    kp = refs[:N_SEL_PAGES]
    vp = refs[N_SEL_PAGES:2 * N_SEL_PAGES]
    o_ref = refs[2 * N_SEL_PAGES]
    b = pl.program_id(0)
    h = pl.program_id(1)
    q = q_ref[0] * (HEAD_DIM ** -0.5)
    slope = jnp.exp2(-(jnp.full((1, 1), h, jnp.int32) + 1).astype(F32))
    t = lax.broadcasted_iota(jnp.int32, (1, PAGE_SIZE), 1).astype(F32)

    s_list = []
    for r in range(N_SEL):
        blk = sel_ref[(b * N_SEL + r) * N_HEADS + h]
        for hf in range(PAGES_PER_BLOCK):
            s = jnp.sum(kp[r * PAGES_PER_BLOCK + hf][...] * q, axis=0, keepdims=True)
            start = (PAST_LEN - blk * BLOCK_A - hf * PAGE_SIZE).astype(F32)
            s_list.append(s - slope * (start - t))
    s_own = jnp.sum(kn_ref[0] * q, axis=0, keepdims=True)

    m = s_own
    for s in s_list:
        m = jnp.maximum(m, jnp.max(s, axis=1, keepdims=True))
    p_own = jnp.exp(s_own - m)
    l = p_own
    acc = p_own * vn_ref[0]
    for i, s in enumerate(s_list):
        p = jnp.exp(s - m)
        l = l + jnp.sum(p, axis=1, keepdims=True)
        acc = acc + jnp.sum(vp[i][...] * p, axis=1, keepdims=True)
    o_ref[0, 0] = acc / l


def _attn_sample(layer, pt_flat, sel_flat, qkv_col, cache_kt, cache_vt):
    def page_spec(r, hf):
        return pl.BlockSpec(
            (None, None, None, HEAD_DIM, PAGE_SIZE),
            lambda b, h, pt, sel: (
                layer,
                pt[b * N_PAGES + sel[(b * N_SEL + r) * N_HEADS + h] * PAGES_PER_BLOCK + hf],
                h, 0, 0))

    def col(part):
        return pl.BlockSpec((1, HEAD_DIM, 1), lambda b, h, pt, sel: (b, part * N_HEADS + h, 0))

    pages = [page_spec(r, hf) for r in range(N_SEL) for hf in range(PAGES_PER_BLOCK)]
    grid_spec = pltpu.PrefetchScalarGridSpec(
        num_scalar_prefetch=2,
        grid=(DEC_BATCH, N_HEADS),
        in_specs=[col(0), col(1), col(2)] + pages + pages,
        out_specs=pl.BlockSpec((1, 1, HEAD_DIM, 1), lambda b, h, pt, sel: (b, h, 0, 0)),
    )
    return pl.pallas_call(
        _sattn_kernel,
        grid_spec=grid_spec,
        out_shape=jax.ShapeDtypeStruct((DEC_BATCH, N_HEADS, HEAD_DIM, 1), F32),
        compiler_params=pltpu.CompilerParams(dimension_semantics=("arbitrary", "arbitrary")),
        name="attn_sample",
    )(pt_flat, sel_flat, qkv_col, qkv_col, qkv_col,
      *([cache_kt] * N_SEL_PAGES), *([cache_vt] * N_SEL_PAGES))


def _smix_kernel(final, x_ref, attn_ref, z_ref, sc0_ref, sc1_ref, sc2_ref, h0_ref, convw_ref,
                 convb_ref, wa_ref, ba_ref, wx_ref, bx_ref, lam_ref, lnw_ref, lnb_ref, ws0_ref,
                 bs0_ref, wba_ref, wbb_ref, wbc_ref, wo_ref, fnw_ref, out_ref, hnew_ref, vn_ref):
    o = 3 * D_A
    xb = z_ref[:, o + R_XB:o + R_XB + D_B]
    cw = convw_ref[...]
    xc = (convb_ref[...] + sc0_ref[...] * cw[0:1] + sc1_ref[...] * cw[1:2]
          + sc2_ref[...] * cw[2:3] + xb * cw[3:4])
    a, b = _lru_gates(xc, wa_ref[...], ba_ref[...], wx_ref[...], bx_ref[...], lam_ref[...])
    hnew = b + a * h0_ref[...]
    hnew_ref[...] = hnew

    vn = _layernorm(_gelu(z_ref[:, o + R_VC:o + R_VC + D_C]), lnw_ref[...], lnb_ref[...])
    vn_ref[...] = vn
    cm = _gelu(z_ref[:, o + R_U:o + R_U + D_C]) * (ws0_ref[...] * vn + bs0_ref[...])

    xn = _merge(x_ref[...], attn_ref[...], hnew, cm,
                z_ref[:, o + R_GA:o + R_GA + D_A], z_ref[:, o + R_GB:o + R_GB + D_B],
                z_ref[:, o + R_GC:o + R_GC + D_C], z_ref[:, o + R_GM:o + R_GM + 3 * D_MODEL],
                wba_ref[...], wbb_ref[...], wbc_ref[...], wo_ref[...])
    out_ref[...] = _rms_rows(xn, fnw_ref[...]) if final else xn


def _mix_sample(final, x, attn, z, sc0, sc1, sc2, h0, lw):
    return pl.pallas_call(
        functools.partial(_smix_kernel, final),
        out_shape=[
            jax.ShapeDtypeStruct((DEC_BATCH, D_MODEL), F32),
            jax.ShapeDtypeStruct((DEC_BATCH, D_B), F32),
            jax.ShapeDtypeStruct((DEC_BATCH, D_C), F32),
        ],
        compiler_params=pltpu.CompilerParams(vmem_limit_bytes=VMEM_LIMIT),
        name="mix_sample",
    )(x, attn, z, sc0, sc1, sc2, h0, lw["conv_w"], lw["conv_b"], lw["wa"], lw["ba"], lw["wx"],
      lw["bx"], lw["lam"], lw["ln_w"], lw["ln_b"], lw["ws0"], lw["bs0"], lw["w_a"], lw["w_b"],
      lw["w_c"], lw["w_o"], lw["fnw"])


def _block_diag(w):
    g, n, _ = w.shape
    eye = jnp.eye(g, dtype=w.dtype)
    return (w[:, :, None, :] * eye[:, None, :, None]).reshape(g * n, g * n)


def _heads_last(xT):
    return xT.reshape(DEPTH, N_HEADS, HEAD_DIM, -1).transpose(0, 3, 1, 2)[:, None]


def kernel(x_prompt, x_sample, cache_k, cache_v, page_table, state_lru_h, state_conv, norm_w, w_in, w_br_a, w_br_b, w_br_c, w_out, conv_w, conv_b, lru_wa, lru_ba, lru_wx, lru_bx, lru_lambda, gmlp_ln_w, gmlp_ln_b, gmlp_ws, gmlp_bs, final_norm_w):
    n_pool = cache_k.shape[1]
    cache_kt = cache_k.transpose(0, 1, 3, 4, 2)
    cache_vt = cache_v.transpose(0, 1, 3, 4, 2)
    cache_kt2 = cache_kt.reshape(DEPTH, n_pool, D_A, PAGE_SIZE)
    pt_flat = page_table.reshape(-1)
    tpos = jnp.where(jnp.arange(LANES)[None, :] < 3, jnp.arange(BLOCK_A, dtype=F32)[:, None], 0.0).astype(BF16)
    fnw = final_norm_w.reshape(1, D_MODEL)

    xp = x_prompt.reshape(SEQ, D_MODEL)
    xs = x_sample.reshape(DEC_BATCH, D_MODEL)
    outs = {k: [] for k in ("ks", "vs", "hp", "cp", "hs", "cs", "cv")}
    kv_stacks = ()
    for l in range(DEPTH):
        final = l == DEPTH - 1
        w_bf = w_in[l].astype(BF16)
        nw = norm_w[l].reshape(1, D_MODEL)
        lw = dict(
            conv_w=conv_w[l], conv_b=conv_b[l].reshape(1, D_B),
            wa=_block_diag(lru_wa[l]).astype(BF16), ba=lru_ba[l].reshape(1, D_B),
            wx=_block_diag(lru_wx[l]).astype(BF16), bx=lru_bx[l].reshape(1, D_B),
            lam=lru_lambda[l].reshape(1, D_B),
            ln_w=gmlp_ln_w[l].reshape(1, D_C), ln_b=gmlp_ln_b[l].reshape(1, D_C),
            ws=gmlp_ws[l], bs_mix=jnp.repeat(gmlp_bs[l].T, D_C // G_C, axis=1),
            ws0=jnp.repeat(gmlp_ws[l][:, 0, 0], D_C // G_C).reshape(1, D_C),
            bs0=jnp.repeat(gmlp_bs[l][:, 0], D_C // G_C).reshape(1, D_C),
            w_a=w_br_a[l].astype(BF16), w_b=w_br_b[l].astype(BF16), w_c=w_br_c[l].astype(BF16),
            w_o=w_out[l].astype(BF16), fnw=fnw)

        z = _proj_sample(xs, nw, w_bf)
        kn = z[:, D_A:2 * D_A]
        vn = z[:, 2 * D_A:3 * D_A]
        qkv_col = z[:, 0:3 * D_A].reshape(DEC_BATCH, 3 * D_A, 1)

        qT, kT, vT, kb, vTb, kmean, rest = _proj_prompt(l, xp, nw, w_bf, kv_stacks)
        kv_stacks = (kT, vT)
        attn, gate_s = _attn_prompt(l, pt_flat, qT, kb, vTb, kmean.reshape(N_BLOCKS, D_A), tpos,
                                    qkv_col, cache_kt2)
        xp, xbtail, htail = _mix_prompt(final, xp, attn, rest, lw)
        outs["hp"].append(htail[7:8])
        outs["cp"].append(xbtail[5:8].reshape(1, 3, D_B))

        sel = _select_sample(gate_s)
        sel_flat = sel[:, :, :, 0].reshape(-1)
        attn_s = _attn_sample(l, pt_flat, sel_flat, qkv_col, cache_kt, cache_vt)
        sc = state_conv[l]
        xs, hnew, cvn = _mix_sample(final, xs, attn_s.reshape(DEC_BATCH, D_A), z,
                                    sc[:, 0], sc[:, 1], sc[:, 2], state_lru_h[l], lw)
        xb_s = z[:, 3 * D_A + R_XB:3 * D_A + R_XB + D_B]
        outs["ks"].append(kn.reshape(DEC_BATCH, 1, N_HEADS, HEAD_DIM))
        outs["vs"].append(vn.reshape(DEC_BATCH, 1, N_HEADS, HEAD_DIM))
        outs["hs"].append(hnew)
        outs["cs"].append(jnp.stack([sc[:, 1], sc[:, 2], xb_s], axis=1))
        outs["cv"].append(cvn.reshape(DEC_BATCH, 1, D_C))

    st = lambda k: jnp.stack(outs[k])
    return (xp.reshape(1, SEQ, D_MODEL), xs.reshape(DEC_BATCH, 1, D_MODEL),
            _heads_last(kv_stacks[0]), _heads_last(kv_stacks[1]), st("ks"), st("vs"), st("hp"), st("cp"), st("hs"), st("cs"), st("cv"))
```
